```python
import jax, jax.numpy as jnp
from jax import lax
import numpy as np

D_MODEL = 1024
BATCH = 16
SEQ = 2048
DEPTH = 2
DEC_BATCH = 32
DEC_SEQ = 4
PAST_LEN = 16384
PAGE_SIZE = 128

ATT_WIDTH = D_MODEL // 2
HEAD_DIM = 64
N_HEADS = ATT_WIDTH // HEAD_DIM
CONV_WIDTH = D_MODEL - ATT_WIDTH
CONV_TAPS = 31
IN_WIDTH = 3 * ATT_WIDTH + 2 * CONV_WIDTH
PLE_DIM = 256
D_FF = 2816
N_EXPERTS = 8
TOP_K = 2
Q_BLOCK = 128
MOE_BLOCK = 128
N_DENSE = (DEPTH + 1) // 2
N_MOE = DEPTH // 2
SB_BIAS_INIT = -7.0
EPS = 1e-6

kernel_name = "hymba_stickbreak_conformer_decoder_step"


def _rmsnorm(x, g):
    xf = x.astype(jnp.float32)
    y = xf * lax.rsqrt(jnp.mean(xf * xf, axis=-1, keepdims=True) + EPS)
    return (y * g.astype(jnp.float32)).astype(x.dtype)


def _layernorm(x, g, b):
    xf = x.astype(jnp.float32)
    mu = jnp.mean(xf, axis=-1, keepdims=True)
    xc = xf - mu
    y = xc * lax.rsqrt(jnp.mean(xc * xc, axis=-1, keepdims=True) + EPS)
    return (y * g.astype(jnp.float32) + b.astype(jnp.float32)).astype(x.dtype)


def _stick_breaking(q, k, v, bias, q_pos, k_pos):
    z = jnp.einsum('bqhd,bkhd->bhqk', q.astype(jnp.float32), k.astype(jnp.float32)) * (HEAD_DIM ** -0.5)
    z = z + bias.astype(jnp.float32)[None, :, None, None]
    mask = (k_pos[None, :] < q_pos[:, None])[None, None]
    log_keep = jnp.where(mask, jax.nn.log_sigmoid(-z), 0.0)
    rev = lax.cumsum(log_keep, axis=3, reverse=True)
    between = jnp.concatenate([rev[..., 1:], jnp.zeros_like(rev[..., :1])], axis=-1)
    a = jnp.where(mask, jnp.exp(jax.nn.log_sigmoid(z) + between), 0.0)
    return jnp.einsum('bhqk,bkhd->bqhd', a, v.astype(jnp.float32)).astype(v.dtype)


def _sb_prompt(q, k, v, bias):
    b, s = q.shape[0], q.shape[1]
    nb = s // Q_BLOCK
    qb = q.reshape(b, nb, Q_BLOCK, N_HEADS, HEAD_DIM).swapaxes(0, 1)
    pos = jnp.arange(s, dtype=jnp.int32)
    out = lax.map(lambda xs: _stick_breaking(xs[0], k, v, bias, xs[1], pos), (qb, pos.reshape(nb, Q_BLOCK)))
    return out.swapaxes(0, 1).reshape(b, s, N_HEADS, HEAD_DIM)


def _project(hn, w_in):
    b, s = hn.shape[0], hn.shape[1]
    proj = hn @ w_in
    q, k, v, a, g = jnp.split(proj, [ATT_WIDTH, 2 * ATT_WIDTH, 3 * ATT_WIDTH, 3 * ATT_WIDTH + CONV_WIDTH], axis=-1)
    heads = lambda t: t.reshape(b, s, N_HEADS, HEAD_DIM)
    return heads(q), heads(k), heads(v), a * jax.nn.sigmoid(g)


def _conv_module(u_ext, conv_w, conv_b, norm_g, norm_b):
    y = lax.conv_general_dilated(u_ext, conv_w[:, None, :], window_strides=(1,), padding='VALID',
                                 dimension_numbers=('NWC', 'WIO', 'NWC'),
                                 feature_group_count=CONV_WIDTH) + conv_b
    return jax.nn.silu(_layernorm(y, norm_g, norm_b))


def _swiglu(x, wg, wu, wd):
    return (jax.nn.silu(x @ wg) * (x @ wu)) @ wd


def _moe_swiglu(x, router, wg, wu, wd):
    t = x.shape[0]
    logits = x.astype(jnp.float32) @ router.astype(jnp.float32)
    top_val, top_idx = lax.top_k(logits, TOP_K)
    gates = jax.nn.softmax(top_val, axis=-1).astype(x.dtype)
    n_assign = t * TOP_K
    flat_e = top_idx.reshape(-1)
    flat_tok = jnp.arange(n_assign, dtype=jnp.int32) // TOP_K
    flat_g = gates.reshape(-1)
    order = jnp.argsort(flat_e)
    sorted_e = flat_e[order]
    counts = jnp.bincount(flat_e, length=N_EXPERTS)
    padded = (counts + MOE_BLOCK - 1) // MOE_BLOCK * MOE_BLOCK
    pad_end = jnp.cumsum(padded)
    pad_start = pad_end - padded
    start = jnp.cumsum(counts) - counts
    dest = pad_start[sorted_e] + jnp.arange(n_assign, dtype=jnp.int32) - start[sorted_e]
    n_blocks = -(-n_assign // MOE_BLOCK) + N_EXPERTS
    rows = n_blocks * MOE_BLOCK
    row_tok = jnp.zeros((rows,), jnp.int32).at[dest].set(flat_tok[order])
    row_gate = jnp.zeros((rows,), x.dtype).at[dest].set(flat_g[order])
    block_e = jnp.minimum(jnp.searchsorted(pad_end, jnp.arange(n_blocks, dtype=jnp.int32) * MOE_BLOCK, side='right'),
                          N_EXPERTS - 1)

    def expert_block(args):
        tok, e = args
        return _swiglu(x[tok], wg[e], wu[e], wd[e])

    yb = lax.map(expert_block, (row_tok.reshape(n_blocks, MOE_BLOCK), block_e))
    y = yb.reshape(rows, x.shape[1]) * row_gate[:, None]
    return jax.ops.segment_sum(y, row_tok, num_segments=t)


def setup_inputs(seed: int = 0) -> dict:
    key = jax.random.key(seed)
    ks = jax.random.split(key, 32)
    nrm = lambda k, shape, scale: jax.random.normal(k, shape, jnp.float32) * scale
    n_pages = PAST_LEN // PAGE_SIZE
    n_used = DEC_BATCH * n_pages
    n_pool = n_used + (n_used + 3) // 4
    page_table = jax.random.permutation(ks[0], n_pool)[:n_used].reshape(DEC_BATCH, n_pages).astype(jnp.int32)
    return {
        "x_prompt": nrm(ks[1], (BATCH, SEQ, D_MODEL), 1.0),
        "x_sample": nrm(ks[2], (DEC_BATCH, DEC_SEQ, D_MODEL), 1.0),
        "cache_k": nrm(ks[3], (DEPTH, n_pool, PAGE_SIZE, N_HEADS, HEAD_DIM), 1.0),
        "cache_v": nrm(ks[4], (DEPTH, n_pool, PAGE_SIZE, N_HEADS, HEAD_DIM), 1.0),
        "state_conv": nrm(ks[5], (DEPTH, DEC_BATCH, CONV_TAPS - 1, CONV_WIDTH), 0.5),
        "page_table": page_table,
        "p_prompt": nrm(ks[6], (DEPTH, BATCH, SEQ, PLE_DIM), 1.0),
        "p_sample": nrm(ks[7], (DEPTH, DEC_BATCH, DEC_SEQ, PLE_DIM), 1.0),
        "w_in": nrm(ks[8], (DEPTH, D_MODEL, IN_WIDTH), D_MODEL ** -0.5),
        "sb_bias": SB_BIAS_INIT + nrm(ks[27], (DEPTH, N_HEADS), 0.1),
        "w_out": nrm(ks[9], (DEPTH, D_MODEL, D_MODEL), D_MODEL ** -0.5),
        "conv_w": nrm(ks[10], (DEPTH, CONV_TAPS, CONV_WIDTH), CONV_TAPS ** -0.5),
        "conv_b": nrm(ks[11], (DEPTH, CONV_WIDTH), 0.01),
        "conv_norm_g": 1.0 + nrm(ks[12], (DEPTH, CONV_WIDTH), 0.1),
        "conv_norm_b": nrm(ks[13], (DEPTH, CONV_WIDTH), 0.01),
        "norm_mix_g": 1.0 + nrm(ks[14], (DEPTH, D_MODEL), 0.1),
        "norm_ffn_g": 1.0 + nrm(ks[15], (DEPTH, D_MODEL), 0.1),
        "norm_ple_g": 1.0 + nrm(ks[16], (DEPTH, D_MODEL), 0.1),
        "w_ple_gate": nrm(ks[17], (DEPTH, D_MODEL, D_MODEL), D_MODEL ** -0.5),
        "w_ple_proj": nrm(ks[18], (DEPTH, PLE_DIM, D_MODEL), PLE_DIM ** -0.5),
        "ffn_w_gate": nrm(ks[19], (N_DENSE, D_MODEL, D_FF), D_MODEL ** -0.5),
        "ffn_w_up": nrm(ks[20], (N_DENSE, D_MODEL, D_FF), D_MODEL ** -0.5),
        "ffn_w_down": nrm(ks[21], (N_DENSE, D_FF, D_MODEL), D_FF ** -0.5),
        "moe_router": nrm(ks[22], (N_MOE, D_MODEL, N_EXPERTS), D_MODEL ** -0.5),
        "moe_w_gate": nrm(ks[23], (N_MOE, N_EXPERTS, D_MODEL, D_FF), D_MODEL ** -0.5),
        "moe_w_up": nrm(ks[24], (N_MOE, N_EXPERTS, D_MODEL, D_FF), D_MODEL ** -0.5),
        "moe_w_down": nrm(ks[25], (N_MOE, N_EXPERTS, D_FF, D_MODEL), D_FF ** -0.5),
        "final_norm_g": 1.0 + nrm(ks[26], (D_MODEL,), 0.1),
    }


def reference(x_prompt, x_sample, cache_k, cache_v, state_conv, page_table, p_prompt, p_sample,
              w_in, sb_bias, w_out, conv_w, conv_b, conv_norm_g, conv_norm_b, norm_mix_g, norm_ffn_g,
              norm_ple_g, w_ple_gate, w_ple_proj, ffn_w_gate, ffn_w_up, ffn_w_down,
              moe_router, moe_w_gate, moe_w_up, moe_w_down, final_norm_g):

    def layer(i, x, p_i, attend, conv_ctx):
        b, s = x.shape[0], x.shape[1]
        hn = _rmsnorm(x, norm_mix_g[i])
        q, k, v, u = _project(hn, w_in[i])
        att = attend(i, q, k, v)
        u_ext = jnp.concatenate([conv_ctx, u], axis=1)
        conv = _conv_module(u_ext, conv_w[i], conv_b[i], conv_norm_g[i], conv_norm_b[i])
        mixed = jnp.concatenate([att.reshape(b, s, ATT_WIDTH), conv], axis=-1)
        x = x + mixed @ w_out[i]
        hn = _rmsnorm(x, norm_ffn_g[i])
        j = i // 2
        if i % 2 == 0:
            f = _swiglu(hn, ffn_w_gate[j], ffn_w_up[j], ffn_w_down[j])
        else:
            f = _moe_swiglu(hn.reshape(b * s, D_MODEL), moe_router[j], moe_w_gate[j],
                            moe_w_up[j], moe_w_down[j]).reshape(b, s, D_MODEL)
        x = x + f
        gate = jax.nn.sigmoid(_rmsnorm(x, norm_ple_g[i]) @ w_ple_gate[i])
        x = x + gate * (p_i @ w_ple_proj[i])
        return x, k, v, u_ext[:, -(CONV_TAPS - 1):]

    def attend_prompt(i, q, k, v):
        return _sb_prompt(q, k, v, sb_bias[i])

    def attend_sample(i, q, k, v):
        db, s = q.shape[0], q.shape[1]
        k_past = cache_k[i][page_table].reshape(db, -1, N_HEADS, HEAD_DIM)
        v_past = cache_v[i][page_table].reshape(db, -1, N_HEADS, HEAD_DIM)
        past = k_past.shape[1]
        k_all = jnp.concatenate([k_past, k], axis=1)
        v_all = jnp.concatenate([v_past, v], axis=1)
        k_pos = jnp.arange(past + s, dtype=jnp.int32)
        q_pos = past + jnp.arange(s, dtype=jnp.int32)
        return _stick_breaking(q, k_all, v_all, sb_bias[i], q_pos, k_pos)

    xp = x_prompt
    xs = x_sample
    conv_zero = jnp.zeros((x_prompt.shape[0], CONV_TAPS - 1, CONV_WIDTH), x_prompt.dtype)
    kp_l, vp_l, cp_l, ks_l, vs_l, cs_l = [], [], [], [], [], []
    for i in range(DEPTH):
        xp, kp, vp, cp = layer(i, xp, p_prompt[i], attend_prompt, conv_zero)
        xs, ksm, vsm, csm = layer(i, xs, p_sample[i], attend_sample, state_conv[i])
        kp_l.append(kp); vp_l.append(vp); cp_l.append(cp)
        ks_l.append(ksm); vs_l.append(vsm); cs_l.append(csm)

    y_prompt = _rmsnorm(xp, final_norm_g)
    y_sample = _rmsnorm(xs, final_norm_g)
    k_prompt = jnp.stack(kp_l)
    v_prompt = jnp.stack(vp_l)
    conv_prompt = jnp.stack(cp_l)
    k_sample = jnp.stack(ks_l)
    v_sample = jnp.stack(vs_l)
    conv_sample = jnp.stack(cs_l)
    return (y_prompt, y_sample, k_prompt, v_prompt, conv_prompt, k_sample, v_sample, conv_sample)
```

```python
import functools

import jax
import jax.numpy as jnp
from jax import lax
from jax.experimental import pallas as pl
from jax.experimental.pallas import tpu as pltpu

F32 = jnp.float32
BF16 = jnp.bfloat16
EPS = 1e-6
HEAD_DIM = 64
LANES = 128
TOP_K = 2
VMEM_LIMIT = 56 * 1024 * 1024
PAGES_PER_STEP = 8
MOE_ROWS = 256
FF_CHUNK = 256


def _params(*sem):
    return pltpu.CompilerParams(dimension_semantics=sem, vmem_limit_bytes=VMEM_LIMIT)


def _tile(n, pref):
    if n <= pref:
        return n
    t = pref - pref % 8
    while n % t:
        t -= 8
    return t


def _dot(a, b):
    return jnp.dot(a, b, preferred_element_type=F32)


def _dot_nt(a, b):
    return lax.dot_general(a, b, (((1,), (1,)), ((), ())), preferred_element_type=F32)


def _rms(xf, g):
    return xf * lax.rsqrt(jnp.mean(xf * xf, axis=-1, keepdims=True) + EPS) * g


def _sigmoid(x):
    return 1.0 / (1.0 + jnp.exp(-x))


def _neg_softplus(z):
    return -(jnp.maximum(z, 0.0) + jnp.log1p(jnp.exp(-jnp.abs(z))))


def _split_bf16(x):
    hi = x.astype(BF16)
    lo = (x - hi.astype(F32)).astype(BF16)
    return hi, lo


def _inproj_kernel(*refs, att, conv, kv_t):
    x_ref, g_ref, w_ref = refs[:3]
    q_ref, k_ref, v_ref, u_ref = refs[-4:]
    hn = _rms(x_ref[...], g_ref[...]).astype(BF16)
    col = lambda c0, n: _dot(hn, w_ref[:, c0:c0 + n])
    q_ref[...] = col(0, att)
    if kv_t:
        kv = _dot_nt(refs[3][...], hn)
        k_ref[...] = kv[:att]
        v_ref[...] = kv[att:]
    else:
        k_ref[...] = col(att, att)
        v_ref[...] = col(2 * att, att)
    a = col(3 * att, conv)
    g = col(3 * att + conv, conv)
    u_ref[...] = a * _sigmoid(g)


def _inproj(x, g, w, att, conv):
    t, d = x.shape
    tm = _tile(t, 512)
    row = lambda n: pl.BlockSpec((tm, n), lambda i: (i, 0))
    full = lambda a: pl.BlockSpec(a.shape, lambda i: (0, 0))
    return pl.pallas_call(
        functools.partial(_inproj_kernel, att=att, conv=conv, kv_t=False),
        grid=(t // tm,),
        in_specs=[row(d), full(g), full(w)],
        out_specs=[row(att), row(att), row(att), row(conv)],
        out_shape=[jax.ShapeDtypeStruct((t, att), F32)] * 3 + [jax.ShapeDtypeStruct((t, conv), F32)],
        compiler_params=_params("parallel"),
        name="inproj",
    )(x, g, w)


def _inproj_kv_t(x, g, w, wkv_t, att, conv, layer, depth, batch, kv_prev):
    t, d = x.shape
    seq = t // batch
    tm = _tile(seq, 512)
    per_seq = seq // tm
    row = lambda n: pl.BlockSpec((tm, n), lambda i: (i, 0))
    full = lambda a: pl.BlockSpec(a.shape, lambda i: (0, 0))
    slab = pl.BlockSpec((None, None, att, tm), lambda i: (layer, i // per_seq, 0, i % per_seq))
    kv_shape = jax.ShapeDtypeStruct((depth, batch, att, seq), F32)
    prev = list(kv_prev) if kv_prev is not None else []
    return pl.pallas_call(
        functools.partial(_inproj_kernel, att=att, conv=conv, kv_t=True),
        grid=(t // tm,),
        in_specs=[row(d), full(g), full(w), full(wkv_t)] + [pl.BlockSpec(memory_space=pl.ANY)] * len(prev),
        out_specs=[row(att), slab, slab, row(conv)],
        out_shape=[jax.ShapeDtypeStruct((t, att), F32), kv_shape, kv_shape, jax.ShapeDtypeStruct((t, conv), F32)],
        input_output_aliases={4: 1, 5: 2} if prev else {},
        compiler_params=_params("parallel"),
        name="inproj_kv_t",
    )(x, g, w, wkv_t, *prev)


def _sb_tile(qx, kb, vb, bias, tri, carry, mask, kv_t):
    z = (_dot(qx, kb) if kv_t else _dot_nt(qx, kb)) + bias
    lk = _neg_softplus(z)
    if mask is not None:
        lk = jnp.where(mask, lk, 0.0)
    hi, lo = _split_bf16(lk)
    later = _dot(hi, tri) + _dot(lo, tri)
    a = jnp.exp(z + lk + later + carry)
    if mask is not None:
        a = jnp.where(mask, a, 0.0)
    a = a.astype(BF16)
    pv = _dot_nt(a, vb) if kv_t else _dot(a, vb)
    return pv, carry + later[:, 0:1] + lk[:, 0:1]


def _sb_prompt_kernel(bias_ref, q_ref, k_ref, v_ref, o_ref, *, tq):
    hp = pl.program_id(1)
    qi = pl.program_id(2)
    lane = lax.broadcasted_iota(jnp.int32, (tq, LANES), 1)
    first = lane < HEAD_DIM
    q = q_ref[...] * (HEAD_DIM ** -0.5)
    qs = (jnp.where(first, q, 0.0).astype(BF16), jnp.where(first, 0.0, q).astype(BF16))
    bias = (bias_ref[2 * hp], bias_ref[2 * hp + 1])
    r = lax.broadcasted_iota(jnp.int32, (tq, tq), 0)
    c = lax.broadcasted_iota(jnp.int32, (tq, tq), 1)
    tri = jnp.where(r > c, 1.0, 0.0).astype(BF16)
    causal = c < r

    def kv(j):
        cols = pl.ds(pl.multiple_of(j * tq, tq), tq)
        return k_ref[:, cols].astype(BF16), v_ref[:, cols].astype(BF16)

    kb, vb = kv(qi)
    state = []
    for h in range(2):
        acc, carry = _sb_tile(qs[h], kb, vb, bias[h], tri, jnp.zeros((tq, 1), F32), causal, True)
        state += [acc, carry]

    def body(n, st):
        kb, vb = kv(qi - 1 - n)
        out = []
        for h in range(2):
            pv, carry = _sb_tile(qs[h], kb, vb, bias[h], tri, st[2 * h + 1], None, True)
            out += [st[2 * h] + pv, carry]
        return tuple(out)

    st = lax.fori_loop(0, qi, body, tuple(state))
    o_ref[...] = jnp.where(first, st[0], st[2])


def _sb_prompt(q, k_t, v_t, layer, bias):
    b, s, w = q.shape
    tq = _tile(s, 256)
    qspec = pl.BlockSpec((None, tq, LANES), lambda bi, hp, qi: (bi, qi, hp))
    kvspec = pl.BlockSpec((None, None, LANES, s), lambda bi, hp, qi: (layer, bi, hp, 0))
    return pl.pallas_call(
        functools.partial(_sb_prompt_kernel, tq=tq),
        grid=(b, w // LANES, s // tq),
        in_specs=[pl.BlockSpec(memory_space=pltpu.SMEM), qspec, kvspec, kvspec],
        out_specs=qspec,
        out_shape=jax.ShapeDtypeStruct((b, s, w), F32),
        compiler_params=_params("parallel", "parallel", "arbitrary"),
        name="sb_prompt",
    )(bias, q, k_t, v_t)


def _sb_paged_kernel(pt_ref, bias_ref, q_ref, kn_ref, vn_ref, *rest, n_heads, n_q, page):
    kpages = rest[:PAGES_PER_STEP]
    vpages = rest[PAGES_PER_STEP:2 * PAGES_PER_STEP]
    o_ref, qbd_ref, acc_ref, carry_ref = rest[2 * PAGES_PER_STEP:]
    step = pl.program_id(1)
    rows = n_q * n_heads
    width = n_heads * HEAD_DIM
    bias = bias_ref[...]
    j = lax.broadcasted_iota(jnp.int32, (page, page), 0)
    s = lax.broadcasted_iota(jnp.int32, (page, page), 1)
    tri = jnp.where(j > s, 1.0, 0.0).astype(BF16)

    @pl.when(step == 0)
    def _init():
        q = q_ref[...] * (HEAD_DIM ** -0.5)
        rep = jnp.concatenate([jnp.broadcast_to(q[t:t + 1, :], (n_heads, width)) for t in range(n_q)], axis=0)
        row = lax.broadcasted_iota(jnp.int32, (rows, width), 0)
        lane = lax.broadcasted_iota(jnp.int32, (rows, width), 1)
        qbd = jnp.where(lane // HEAD_DIM == row % n_heads, rep, 0.0).astype(BF16)
        qbd_ref[...] = qbd
        pad = jnp.zeros((page - n_q, width), F32)
        kn = jnp.concatenate([kn_ref[...], pad], axis=0).astype(BF16)
        vn = jnp.concatenate([vn_ref[...], pad], axis=0).astype(BF16)
        key = lax.broadcasted_iota(jnp.int32, (rows, page), 1)
        t = lax.broadcasted_iota(jnp.int32, (rows, page), 0) // n_heads
        acc, carry = _sb_tile(qbd, kn, vn, bias, tri, jnp.zeros((rows, 1), F32), key < t, False)
        acc_ref[...] = acc
        carry_ref[...] = carry

    qbd = qbd_ref[...]
    for p in reversed(range(PAGES_PER_STEP)):
        kb = kpages[p][...].astype(BF16)
        vb = vpages[p][...].astype(BF16)
        pv, carry = _sb_tile(qbd, kb, vb, bias, tri, carry_ref[...], None, True)
        acc_ref[...] += pv
        carry_ref[...] = carry

    @pl.when(step == pl.num_programs(1) - 1)
    def _finish():
        row = lax.broadcasted_iota(jnp.int32, (rows, width), 0)
        lane = lax.broadcasted_iota(jnp.int32, (rows, width), 1)
        own = jnp.where(lane // HEAD_DIM == row % n_heads, acc_ref[...], 0.0)
        o_ref[...] = jnp.sum(own.reshape(n_q, n_heads, width), axis=1)


def _sb_paged(q, k_new, v_new, cache_kt, cache_vt, layer, page_table, bias):
    db, n_q, width = q.shape
    n_heads = width // HEAD_DIM
    page = cache_kt.shape[3]
    n_pages = page_table.shape[1]
    steps = n_pages // PAGES_PER_STEP
    rows = n_q * n_heads
    bias_col = jnp.tile(bias, n_q).reshape(rows, 1)

    def page_spec(p):
        def index(bi, si, pt):
            return (layer, pt[bi, (steps - 1 - si) * PAGES_PER_STEP + p], 0, 0)
        return pl.BlockSpec((None, None, width, page), index)

    small = pl.BlockSpec((None, n_q, width), lambda bi, si, pt: (bi, 0, 0))
    pages = [page_spec(p) for p in range(PAGES_PER_STEP)]
    grid_spec = pltpu.PrefetchScalarGridSpec(
        num_scalar_prefetch=1,
        grid=(db, steps),
        in_specs=[pl.BlockSpec((rows, 1), lambda bi, si, pt: (0, 0)), small, small, small] + pages + pages,
        out_specs=small,
        scratch_shapes=[pltpu.VMEM((rows, width), BF16), pltpu.VMEM((rows, width), F32),
                        pltpu.VMEM((rows, 1), F32)],
    )
    return pl.pallas_call(
        functools.partial(_sb_paged_kernel, n_heads=n_heads, n_q=n_q, page=page),
        grid_spec=grid_spec,
        out_shape=jax.ShapeDtypeStruct((db, n_q, width), F32),
        compiler_params=_params("parallel", "arbitrary"),
        name="sb_paged",
    )(page_table, bias_col, q, k_new, v_new, *([cache_kt] * PAGES_PER_STEP), *([cache_vt] * PAGES_PER_STEP))


CONV_PAD = 32


def _conv_kernel(ctx_ref, u_ref, w_ref, b_ref, g_ref, nb_ref, o_ref, ext_ref, *, taps, seq, rows):
    n_ctx = taps - 1
    off = CONV_PAD - n_ctx
    ext_ref[0:CONV_PAD, :] = jnp.zeros((CONV_PAD, ext_ref.shape[1]), F32)
    ext_ref[off:CONV_PAD, :] = ctx_ref[...]
    if seq % 8:
        ext_ref[CONV_PAD:, :] = jnp.zeros((ext_ref.shape[0] - CONV_PAD, ext_ref.shape[1]), F32)
    ext_ref[CONV_PAD:CONV_PAD + seq, :] = u_ref[...]
    w = w_ref[...]
    out_rows = min(rows, seq)

    def chunk(i, _):
        base = pl.multiple_of(i * rows, rows)
        win = ext_ref[pl.ds(base, rows + CONV_PAD), :]
        acc = jnp.zeros((rows, ext_ref.shape[1]), F32) + b_ref[...]
        for k in range(taps):
            acc = acc + w[k:k + 1, :] * win[off + k:off + k + rows]
        mu = jnp.mean(acc, axis=-1, keepdims=True)
        xc = acc - mu
        y = xc * lax.rsqrt(jnp.mean(xc * xc, axis=-1, keepdims=True) + EPS) * g_ref[...] + nb_ref[...]
        y = y * _sigmoid(y)
        o_ref[pl.ds(base, out_rows), :] = y[:out_rows]
        return 0

    lax.fori_loop(0, max(seq // rows, 1), chunk, 0)


def _conv_module(ctx, u, w, b, g, nb):
    bsz, seq, ch = u.shape
    taps = w.shape[0]
    rows = 32 if seq >= 32 else 8
    ext_rows = CONV_PAD + max(seq, rows)
    vec = lambda a: pl.BlockSpec(a.shape, lambda i: (0, 0))
    return pl.pallas_call(
        functools.partial(_conv_kernel, taps=taps, seq=seq, rows=rows),
        grid=(bsz,),
        in_specs=[pl.BlockSpec((None, taps - 1, ch), lambda i: (i, 0, 0)),
                  pl.BlockSpec((None, seq, ch), lambda i: (i, 0, 0)),
                  vec(w), vec(b), vec(g), vec(nb)],
        out_specs=pl.BlockSpec((None, seq, ch), lambda i: (i, 0, 0)),
        out_shape=jax.ShapeDtypeStruct((bsz, seq, ch), F32),
        scratch_shapes=[pltpu.VMEM((ext_rows, ch), F32)],
        compiler_params=_params("parallel"),
        name="conv_module",
    )(ctx, u, w, b, g, nb)


def _outproj_kernel(x_ref, att_ref, conv_ref, wa_ref, wc_ref, o_ref):
    o_ref[...] = (x_ref[...] + _dot(att_ref[...].astype(BF16), wa_ref[...])
                  + _dot(conv_ref[...].astype(BF16), wc_ref[...]))


def _outproj(x, att, conv, wa, wc):
    t, d = x.shape
    tm = _tile(t, 512)
    row = lambda a: pl.BlockSpec((tm, a.shape[1]), lambda i: (i, 0))
    full = lambda a: pl.BlockSpec(a.shape, lambda i: (0, 0))
    return pl.pallas_call(
        _outproj_kernel,
        grid=(t // tm,),
        in_specs=[row(x), row(att), row(conv), full(wa), full(wc)],
        out_specs=row(x),
        out_shape=jax.ShapeDtypeStruct((t, d), F32),
        compiler_params=_params("parallel"),
        name="outproj",
    )(x, att, conv, wa, wc)


def _swiglu_hidden(hn, wg_ref, wu_ref, h_ref):
    d_ff = h_ref.shape[1]
    for c0 in range(0, d_ff, FF_CHUNK):
        g = _dot(hn, wg_ref[:, c0:c0 + FF_CHUNK])
        u = _dot(hn, wu_ref[:, c0:c0 + FF_CHUNK])
        h_ref[:, c0:c0 + FF_CHUNK] = (g * _sigmoid(g) * u).astype(BF16)


def _ffn_kernel(x_ref, g_ref, wg_ref, wu_ref, wd_ref, o_ref, h_ref):
    x = x_ref[...]
    _swiglu_hidden(_rms(x, g_ref[...]).astype(BF16), wg_ref, wu_ref, h_ref)
    o_ref[...] = x + _dot(h_ref[...], wd_ref[...])


def _resident(shape, index):
    return pl.BlockSpec(shape, index, pipeline_mode=pl.Buffered(1))


def _ffn(x, g, wg, wu, wd):
    t, d = x.shape
    d_ff = wg.shape[1]
    tm = _tile(t, 512)
    row = pl.BlockSpec((tm, d), lambda i: (i, 0))
    return pl.pallas_call(
        _ffn_kernel,
        grid=(t // tm,),
        in_specs=[row, pl.BlockSpec(g.shape, lambda i: (0, 0)),
                  _resident(wg.shape, lambda i: (0, 0)), _resident(wu.shape, lambda i: (0, 0)),
                  _resident(wd.shape, lambda i: (0, 0))],
        out_specs=row,
        out_shape=jax.ShapeDtypeStruct((t, d), F32),
        scratch_shapes=[pltpu.VMEM((tm, d_ff), BF16)],
        compiler_params=_params("parallel"),
        name="ffn_dense",
    )(x, g, wg, wu, wd)


def _expert_kernel(be_ref, x_ref, gate_ref, wg_ref, wu_ref, wd_ref, o_ref, h_ref):
    _swiglu_hidden(x_ref[...], wg_ref, wu_ref, h_ref)
    o_ref[...] = _dot(h_ref[...], wd_ref[...]) * gate_ref[...]


def _expert_blocks(xg, row_gate, block_e, wg, wu, wd):
    rows, d = xg.shape
    d_ff = wg.shape[2]
    grid_spec = pltpu.PrefetchScalarGridSpec(
        num_scalar_prefetch=1,
        grid=(rows // MOE_ROWS,),
        in_specs=[pl.BlockSpec((MOE_ROWS, d), lambda i, be: (i, 0)),
                  pl.BlockSpec((MOE_ROWS, 1), lambda i, be: (i, 0)),
                  _resident((None, d, d_ff), lambda i, be: (be[i], 0, 0)),
                  _resident((None, d, d_ff), lambda i, be: (be[i], 0, 0)),
                  _resident((None, d_ff, d), lambda i, be: (be[i], 0, 0))],
        out_specs=pl.BlockSpec((MOE_ROWS, d), lambda i, be: (i, 0)),
        scratch_shapes=[pltpu.VMEM((MOE_ROWS, d_ff), BF16)],
    )
    return pl.pallas_call(
        _expert_kernel,
        grid_spec=grid_spec,
        out_shape=jax.ShapeDtypeStruct((rows, d), F32),
        compiler_params=_params("arbitrary"),
        name="moe_experts",
    )(block_e, xg, row_gate, wg, wu, wd)


def _router_kernel(x_ref, g_ref, r_ref, hn_ref, route_ref, *, n_experts):
    hn = _rms(x_ref[...], g_ref[...])
    hn_ref[...] = hn.astype(BF16)
    h_hi, h_lo = _split_bf16(hn)
    r_hi, r_lo = _split_bf16(r_ref[...])
    logits = _dot(h_hi, r_hi) + (_dot(h_hi, r_lo) + _dot(h_lo, r_hi))
    lane = lax.broadcasted_iota(jnp.int32, logits.shape, 1).astype(F32)
    neg = jnp.float32(-jnp.inf)
    logits = jnp.where(lane < n_experts, logits, neg)
    v1 = jnp.max(logits, axis=-1, keepdims=True)
    i1 = jnp.min(jnp.where(logits == v1, lane, float(LANES)), axis=-1, keepdims=True)
    rest = jnp.where(lane == i1, neg, logits)
    v2 = jnp.max(rest, axis=-1, keepdims=True)
    i2 = jnp.min(jnp.where(rest == v2, lane, float(LANES)), axis=-1, keepdims=True)
    e = jnp.exp(v2 - v1)
    g1 = 1.0 / (1.0 + e)
    g2 = e / (1.0 + e)
    route_ref[...] = jnp.where(lane == 0, i1, jnp.where(lane == 1, i2, jnp.where(lane == 2, g1,
                               jnp.where(lane == 3, g2, 0.0))))


def _router(x, g, router_pad, n_experts):
    t, d = x.shape
    tm = _tile(t, 512)
    row = lambda n: pl.BlockSpec((tm, n), lambda i: (i, 0))
    return pl.pallas_call(
        functools.partial(_router_kernel, n_experts=n_experts),
        grid=(t // tm,),
        in_specs=[row(d), pl.BlockSpec(g.shape, lambda i: (0, 0)),
                  pl.BlockSpec(router_pad.shape, lambda i: (0, 0))],
        out_specs=[row(d), row(LANES)],
        out_shape=[jax.ShapeDtypeStruct((t, d), BF16), jax.ShapeDtypeStruct((t, LANES), F32)],
        compiler_params=_params("parallel"),
        name="router",
    )(x, g, router_pad)


def _moe(x, g, router, wg, wu, wd):
    t, d = x.shape
    n_experts = router.shape[1]
    router_pad = jnp.pad(router, ((0, 0), (0, LANES - n_experts)))
    hn, route = _router(x, g, router_pad, n_experts)
    flat_e = route[:, :TOP_K].astype(jnp.int32).reshape(-1)
    flat_g = route[:, TOP_K:2 * TOP_K].reshape(-1)
    n_assign = t * TOP_K
    onehot = (flat_e[:, None] == jnp.arange(n_experts, dtype=jnp.int32)[None, :]).astype(jnp.int32)
    before = jnp.cumsum(onehot, axis=0) - onehot
    counts = jnp.sum(onehot, axis=0)
    padded = (counts + MOE_ROWS - 1) // MOE_ROWS * MOE_ROWS
    pad_end = jnp.cumsum(padded)
    pad_start = pad_end - padded
    dest = jnp.sum(onehot * (before + pad_start[None, :]), axis=1)
    n_blocks = -(-n_assign // MOE_ROWS) + n_experts
    rows = n_blocks * MOE_ROWS
    flat_tok = jnp.arange(n_assign, dtype=jnp.int32) // TOP_K
    row_tok = jnp.zeros((rows,), jnp.int32).at[dest].set(flat_tok)
    row_gate = jnp.zeros((rows,), F32).at[dest].set(flat_g)
    block_e = jnp.minimum(jnp.searchsorted(pad_end, jnp.arange(n_blocks, dtype=jnp.int32) * MOE_ROWS,
                                           side='right'), n_experts - 1).astype(jnp.int32)
    y = _expert_blocks(hn[row_tok], row_gate[:, None], block_e, wg, wu, wd)
    pos = dest.reshape(t, TOP_K)
    return y[pos[:, 0]], y[pos[:, 1]]


def _ple_kernel(*refs, n_add, final):
    x_ref = refs[0]
    adds = refs[1:1 + n_add]
    p_ref, g_ref, wgate_ref, wproj_ref = refs[1 + n_add:5 + n_add]
    fg_ref = refs[5 + n_add] if final else None
    o_ref = refs[-1]
    x = x_ref[...]
    for a in adds:
        x = x + a[...]
    gate = _sigmoid(_dot(_rms(x, g_ref[...]).astype(BF16), wgate_ref[...]))
    x = x + gate * _dot(p_ref[...].astype(BF16), wproj_ref[...])
    o_ref[...] = _rms(x, fg_ref[...]) if final else x


def _ple(x, adds, p, g, wgate, wproj, final_g):
    t, d = x.shape
    tm = _tile(t, 512)
    row = lambda a: pl.BlockSpec((tm, a.shape[1]), lambda i: (i, 0))
    full = lambda a: pl.BlockSpec(a.shape, lambda i: (0, 0))
    final = final_g is not None
    ops = [x, *adds, p, g, wgate, wproj] + ([final_g] if final else [])
    specs = [row(x)] + [row(a) for a in adds] + [row(p), full(g), full(wgate), full(wproj)]
    specs += [full(final_g)] if final else []
    return pl.pallas_call(
        functools.partial(_ple_kernel, n_add=len(adds), final=final),
        grid=(t // tm,),
        in_specs=specs,
        out_specs=row(x),
        out_shape=jax.ShapeDtypeStruct((t, d), F32),
        compiler_params=_params("parallel"),
        name="ple",
    )(*ops)


def kernel(x_prompt, x_sample, cache_k, cache_v, state_conv, page_table, p_prompt, p_sample,
           w_in, sb_bias, w_out, conv_w, conv_b, conv_norm_g, conv_norm_b, norm_mix_g, norm_ffn_g,
           norm_ple_g, w_ple_gate, w_ple_proj, ffn_w_gate, ffn_w_up, ffn_w_down,
           moe_router, moe_w_gate, moe_w_up, moe_w_down, final_norm_g):
    depth, d_model = norm_mix_g.shape
    n_heads = sb_bias.shape[1]
    att = n_heads * HEAD_DIM
    conv = conv_w.shape[2]
    taps = conv_w.shape[1]
    n_pool, page = cache_k.shape[1], cache_k.shape[2]
    cache_kt = jnp.transpose(cache_k, (0, 1, 3, 4, 2)).reshape(depth, n_pool, att, page)
    cache_vt = jnp.transpose(cache_v, (0, 1, 3, 4, 2)).reshape(depth, n_pool, att, page)
    vec = lambda a: a.reshape(1, -1)
    final_g = vec(final_norm_g)
    bsz, seq, _ = x_prompt.shape

    def mix(i, x, a, u3, ctx, p3):
        b, s, _ = u3.shape
        c3 = _conv_module(ctx, u3, conv_w[i], vec(conv_b[i]), vec(conv_norm_g[i]), vec(conv_norm_b[i]))
        wo = w_out[i].astype(BF16)
        x = _outproj(x, a, c3.reshape(b * s, conv), wo[:att], wo[att:])
        j = i // 2
        if i % 2 == 0:
            x = _ffn(x, vec(norm_ffn_g[i]), ffn_w_gate[j].astype(BF16), ffn_w_up[j].astype(BF16),
                     ffn_w_down[j].astype(BF16))
            adds = []
        else:
            adds = list(_moe(x, vec(norm_ffn_g[i]), moe_router[j], moe_w_gate[j].astype(BF16),
                             moe_w_up[j].astype(BF16), moe_w_down[j].astype(BF16)))
        x = _ple(x, adds, p3.reshape(b * s, -1), vec(norm_ple_g[i]), w_ple_gate[i].astype(BF16),
                 w_ple_proj[i].astype(BF16), final_g if i == depth - 1 else None)
        return x, jnp.concatenate([ctx, u3], axis=1)[:, -(taps - 1):]

    xp = x_prompt.reshape(bsz * seq, d_model)
    db, n_q, _ = x_sample.shape
    xs = x_sample.reshape(db * n_q, d_model)
    conv_zero = jnp.zeros((bsz, taps - 1, conv), F32)
    kv_t = None
    cp_l, ks_l, vs_l, cs_l = [], [], [], []
    for i in range(depth):
        g = vec(norm_mix_g[i])
        w = w_in[i].astype(BF16)
        wkv_t = w_in[i][:, att:3 * att].T.astype(BF16)
        q, k_t, v_t, u = _inproj_kv_t(xp, g, w, wkv_t, att, conv, i, depth, bsz, kv_t)
        kv_t = (k_t, v_t)
        a = _sb_prompt(q.reshape(bsz, seq, att), k_t, v_t, i, sb_bias[i]).reshape(bsz * seq, att)
        xp, cp = mix(i, xp, a, u.reshape(bsz, seq, conv), conv_zero, p_prompt[i])

        q, k, v, u = _inproj(xs, g, w, att, conv)
        q3, k3, v3 = (t.reshape(db, n_q, att) for t in (q, k, v))
        a = _sb_paged(q3, k3, v3, cache_kt, cache_vt, i, page_table, sb_bias[i]).reshape(db * n_q, att)
        xs, cs = mix(i, xs, a, u.reshape(db, n_q, conv), state_conv[i], p_sample[i])
        cp_l.append(cp)
        ks_l.append(k3.reshape(db, n_q, n_heads, HEAD_DIM))
        vs_l.append(v3.reshape(db, n_q, n_heads, HEAD_DIM))
        cs_l.append(cs)

    heads_last = lambda t: jnp.transpose(t.reshape(depth, bsz, n_heads, HEAD_DIM, seq), (0, 1, 4, 2, 3))
    return (xp.reshape(bsz, seq, d_model), xs.reshape(db, n_q, d_model), heads_last(kv_t[0]), heads_last(kv_t[1]),
            jnp.stack(cp_l), jnp.stack(ks_l), jnp.stack(vs_l), jnp.stack(cs_l))
```

```python
import functools

import jax
import jax.numpy as jnp
from jax import lax
from jax.experimental import pallas as pl
from jax.experimental.pallas import tpu as pltpu

F32 = jnp.float32
BF16 = jnp.bfloat16
MX = jnp.bfloat16
EPS = 1e-6
HEAD_DIM = 64
LANES = 128
TOP_K = 2
VMEM_LIMIT = 56 * 1024 * 1024
PAGES_PER_STEP = 8
MOE_ROWS = 256
FF_CHUNK = 256


def _params(*sem):
    return pltpu.CompilerParams(dimension_semantics=sem, vmem_limit_bytes=VMEM_LIMIT)


def _tile(n, pref):
    if n <= pref:
        return n
    t = pref - pref % 8
    while n % t:
        t -= 8
    return t


def _dot(a, b):
    return jnp.dot(a, b, preferred_element_type=F32)


def _dot_nt(a, b):
    return lax.dot_general(a, b, (((1,), (1,)), ((), ())), preferred_element_type=F32)


def _rms(xf, g):
    return xf * lax.rsqrt(jnp.mean(xf * xf, axis=-1, keepdims=True) + EPS) * g


def _sigmoid(x):
    return 1.0 / (1.0 + jnp.exp(-x))


def _split_bf16(x):
    hi = x.astype(BF16)
    lo = (x - hi.astype(F32)).astype(BF16)
    return hi, lo


def _inproj_kernel(*refs, att, conv, kv_t):
    x_ref, g_ref, w_ref = refs[:3]
    q_ref, k_ref, v_ref, u_ref = refs[-4:]
    hn = _rms(x_ref[...], g_ref[...]).astype(BF16)
    col = lambda c0, n: _dot(hn, w_ref[:, c0:c0 + n])
    q_ref[...] = col(0, att)
    if kv_t:
        kv = _dot_nt(refs[3][...], hn)
        k_ref[...] = kv[:att]
        v_ref[...] = kv[att:]
    else:
        k_ref[...] = col(att, att)
        v_ref[...] = col(2 * att, att)
    a = col(3 * att, conv)
    g = col(3 * att + conv, conv)
    u_ref[...] = a * _sigmoid(g)


def _inproj(x, g, w, att, conv):
    t, d = x.shape
    tm = _tile(t, 512)
    row = lambda n: pl.BlockSpec((tm, n), lambda i: (i, 0))
    full = lambda a: pl.BlockSpec(a.shape, lambda i: (0, 0))
    return pl.pallas_call(
        functools.partial(_inproj_kernel, att=att, conv=conv, kv_t=False),
        grid=(t // tm,),
        in_specs=[row(d), full(g), full(w)],
        out_specs=[row(att), row(att), row(att), row(conv)],
        out_shape=[jax.ShapeDtypeStruct((t, att), F32)] * 3 + [jax.ShapeDtypeStruct((t, conv), F32)],
        compiler_params=_params("parallel"),
        name="inproj",
    )(x, g, w)


def _inproj_kv_t(x, g, w, wkv_t, att, conv, layer, depth, batch, kv_prev):
    t, d = x.shape
    seq = t // batch
    tm = _tile(seq, 512)
    per_seq = seq // tm
    row = lambda n: pl.BlockSpec((tm, n), lambda i: (i, 0))
    full = lambda a: pl.BlockSpec(a.shape, lambda i: (0, 0))
    slab = pl.BlockSpec((None, None, att, tm), lambda i: (layer, i // per_seq, 0, i % per_seq))
    kv_shape = jax.ShapeDtypeStruct((depth, batch, att, seq), F32)
    prev = list(kv_prev) if kv_prev is not None else []
    return pl.pallas_call(
        functools.partial(_inproj_kernel, att=att, conv=conv, kv_t=True),
        grid=(t // tm,),
        in_specs=[row(d), full(g), full(w), full(wkv_t)] + [pl.BlockSpec(memory_space=pl.ANY)] * len(prev),
        out_specs=[row(att), slab, slab, row(conv)],
        out_shape=[jax.ShapeDtypeStruct((t, att), F32), kv_shape, kv_shape, jax.ShapeDtypeStruct((t, conv), F32)],
        input_output_aliases={4: 1, 5: 2} if prev else {},
        compiler_params=_params("parallel"),
        name="inproj_kv_t",
    )(x, g, w, wkv_t, *prev)


def _log_sigmoids(z):
    ls = jnp.minimum(z, 0.0) - jnp.log(1.0 + jnp.exp(-jnp.abs(z)))
    return ls, ls - z


def _later_than(n):
    j = lax.broadcasted_iota(jnp.int32, (n, n), 0)
    s = lax.broadcasted_iota(jnp.int32, (n, n), 1)
    return jnp.where(j > s, 1.0, 0.0).astype(MX)


def _sb_weights(z, tri, carry, mask):
    ls, lk = _log_sigmoids(z)
    if mask is not None:
        lk = jnp.where(mask, lk, 0.0)
    later = _dot(lk.astype(MX), tri)
    a = jnp.exp(ls + later + carry)
    if mask is not None:
        a = jnp.where(mask, a, 0.0)
    return a, carry + later[:, 0:1] + lk[:, 0:1]


def _sb_prompt_kernel(bias_ref, q_ref, k_ref, v_ref, o_ref, *, tq):
    hp = pl.program_id(1)
    qi = pl.program_id(2)
    first = lax.broadcasted_iota(jnp.int32, (tq, LANES), 1) < HEAD_DIM
    q = q_ref[...] * (HEAD_DIM ** -0.5)
    q2 = jnp.concatenate([jnp.where(first, q, 0.0), jnp.where(first, 0.0, q)], axis=0).astype(MX)
    bias_a = bias_ref[2 * hp]
    bias_b = bias_ref[2 * hp + 1]
    tri = _later_than(tq)
    r = lax.broadcasted_iota(jnp.int32, (2 * tq, tq), 0)
    c = lax.broadcasted_iota(jnp.int32, (2 * tq, tq), 1)
    causal = c < jnp.where(r < tq, r, r - tq)
    v_first = lax.broadcasted_iota(jnp.int32, (LANES, tq), 0) < HEAD_DIM

    def cols(j):
        return pl.ds(pl.multiple_of(j * tq, tq), tq)

    def logits(j):
        z = _dot(q2, k_ref[:, cols(j)].astype(MX))
        return jnp.concatenate([z[:tq] + bias_a, z[tq:] + bias_b], axis=0)

    def weighted_values(a, j):
        vb = v_ref[:, cols(j)]
        v2 = jnp.concatenate([jnp.where(v_first, vb, 0.0), jnp.where(v_first, 0.0, vb)], axis=1).astype(MX)
        a2 = jnp.concatenate([a[:tq], a[tq:]], axis=1)
        return _dot_nt(a2, v2)

    a, carry = _sb_weights(logits(qi), tri, jnp.zeros((2 * tq, 1), F32), causal)
    state = (jnp.zeros((tq, LANES), F32), carry, logits(jnp.maximum(qi - 1, 0)), a.astype(MX))

    def body(n, st):
        acc, carry, z, a_prev = st
        j = qi - 1 - n
        pv = weighted_values(a_prev, j + 1)
        z_next = logits(jnp.maximum(j - 1, 0))
        a, carry = _sb_weights(z, tri, carry, None)
        return acc + pv, carry, z_next, a.astype(MX)

    acc, _, _, a_last = lax.fori_loop(0, qi, body, state)
    o_ref[...] = acc + weighted_values(a_last, 0)


def _sb_prompt(q, k_t, v_t, layer, bias):
    b, s, w = q.shape
    tq = _tile(s, 256)
    qspec = pl.BlockSpec((None, tq, LANES), lambda bi, hp, qi: (bi, qi, hp))
    kvspec = pl.BlockSpec((None, None, LANES, s), lambda bi, hp, qi: (layer, bi, hp, 0))
    return pl.pallas_call(
        functools.partial(_sb_prompt_kernel, tq=tq),
        grid=(b, w // LANES, s // tq),
        in_specs=[pl.BlockSpec(memory_space=pltpu.SMEM), qspec, kvspec, kvspec],
        out_specs=qspec,
        out_shape=jax.ShapeDtypeStruct((b, s, w), F32),
        compiler_params=_params("parallel", "parallel", "arbitrary"),
        name="sb_prompt",
    )(bias, q, k_t, v_t)


def _sb_paged_kernel(pt_ref, bias_ref, q_ref, kn_ref, vn_ref, *rest, n_heads, n_q, page):
    kpages = rest[:PAGES_PER_STEP]
    vpages = rest[PAGES_PER_STEP:2 * PAGES_PER_STEP]
    o_ref, qbd_ref, acc_ref, new_ref, carry_ref = rest[2 * PAGES_PER_STEP:]
    step = pl.program_id(1)
    rows = n_q * n_heads
    width = n_heads * HEAD_DIM
    pairs = PAGES_PER_STEP // 2
    bias = bias_ref[...]
    tri = _later_than(2 * page)

    @pl.when(step == 0)
    def _init():
        q = q_ref[...] * (HEAD_DIM ** -0.5)
        rep = jnp.concatenate([jnp.broadcast_to(q[t:t + 1, :], (n_heads, width)) for t in range(n_q)], axis=0)
        row = lax.broadcasted_iota(jnp.int32, (rows, width), 0)
        lane = lax.broadcasted_iota(jnp.int32, (rows, width), 1)
        qbd = jnp.where(lane // HEAD_DIM == row % n_heads, rep, 0.0)
        qbd_ref[...] = qbd
        pad = jnp.zeros((page - n_q, width), F32)
        kn = jnp.concatenate([kn_ref[...], pad], axis=0)
        vn = jnp.concatenate([vn_ref[...], pad], axis=0)
        key = lax.broadcasted_iota(jnp.int32, (rows, page), 1)
        t = lax.broadcasted_iota(jnp.int32, (rows, page), 0) // n_heads
        z = _dot_nt(qbd.astype(MX), kn.astype(MX)) + bias
        a, carry = _sb_weights(z, tri[:page, :page], jnp.zeros((rows, 1), F32), key < t)
        new_ref[...] = _dot(a.astype(MX), vn.astype(MX))
        carry_ref[...] = carry
        acc_ref[...] = jnp.zeros(acc_ref.shape, F32)

    qbd = qbd_ref[...].astype(MX)
    scores = []
    for i in range(pairs):
        keys = jnp.concatenate([kpages[2 * i][...], kpages[2 * i + 1][...]], axis=1)
        scores.append(_log_sigmoids(_dot(qbd, keys.astype(MX)) + bias))
    later = _dot(jnp.concatenate([lk for _, lk in scores], axis=0).astype(MX), tri)
    carry = carry_ref[...]
    weights = [None] * pairs
    for i in reversed(range(pairs)):
        ls, lk = scores[i]
        lt = later[i * rows:(i + 1) * rows]
        weights[i] = jnp.exp(ls + lt + carry)
        carry = carry + lt[:, 0:1] + lk[:, 0:1]
    carry_ref[...] = carry
    values = jnp.concatenate([vpages[p][...] for p in range(PAGES_PER_STEP)], axis=1)
    acc_ref[...] += _dot_nt(values.astype(MX), jnp.concatenate(weights, axis=1).astype(MX))

    @pl.when(step == pl.num_programs(1) - 1)
    def _finish():
        r = lax.broadcasted_iota(jnp.int32, (width, rows), 0)
        c = lax.broadcasted_iota(jnp.int32, (width, rows), 1)
        hi, lo = _split_bf16(jnp.where(c % n_heads == r // HEAD_DIM, acc_ref[...], 0.0))
        t = lax.broadcasted_iota(jnp.int32, (8, rows), 0)
        cc = lax.broadcasted_iota(jnp.int32, (8, rows), 1)
        pick = jnp.where(cc // n_heads == t, 1.0, 0.0).astype(BF16)
        past = _dot_nt(pick, hi) + _dot_nt(pick, lo)
        row = lax.broadcasted_iota(jnp.int32, (rows, width), 0)
        lane = lax.broadcasted_iota(jnp.int32, (rows, width), 1)
        own = jnp.where(lane // HEAD_DIM == row % n_heads, new_ref[...], 0.0)
        o_ref[...] = past[:n_q] + jnp.sum(own.reshape(n_q, n_heads, width), axis=1)


def _sb_paged(q, k_new, v_new, cache_kt, cache_vt, layer, page_table, bias):
    db, n_q, width = q.shape
    n_heads = width // HEAD_DIM
    page = cache_kt.shape[3]
    n_pages = page_table.shape[1]
    steps = n_pages // PAGES_PER_STEP
    rows = n_q * n_heads
    bias_col = jnp.tile(bias, n_q).reshape(rows, 1)

    def page_spec(p):
        def index(bi, si, pt):
            return (layer, pt[bi, (steps - 1 - si) * PAGES_PER_STEP + p], 0, 0)
        return pl.BlockSpec((None, None, width, page), index)

    small = pl.BlockSpec((None, n_q, width), lambda bi, si, pt: (bi, 0, 0))
    pages = [page_spec(p) for p in range(PAGES_PER_STEP)]
    grid_spec = pltpu.PrefetchScalarGridSpec(
        num_scalar_prefetch=1,
        grid=(db, steps),
        in_specs=[pl.BlockSpec((rows, 1), lambda bi, si, pt: (0, 0)), small, small, small] + pages + pages,
        out_specs=small,
        scratch_shapes=[pltpu.VMEM((rows, width), F32), pltpu.VMEM((width, rows), F32),
                        pltpu.VMEM((rows, width), F32), pltpu.VMEM((rows, 1), F32)],
    )
    return pl.pallas_call(
        functools.partial(_sb_paged_kernel, n_heads=n_heads, n_q=n_q, page=page),
        grid_spec=grid_spec,
        out_shape=jax.ShapeDtypeStruct((db, n_q, width), F32),
        compiler_params=_params("parallel", "arbitrary"),
        name="sb_paged",
    )(page_table, bias_col, q, k_new, v_new, *([cache_kt] * PAGES_PER_STEP), *([cache_vt] * PAGES_PER_STEP))


CONV_PAD = 32


def _conv_kernel(ctx_ref, u_ref, w_ref, b_ref, g_ref, nb_ref, o_ref, ext_ref, *, taps, seq, rows):
    n_ctx = taps - 1
    off = CONV_PAD - n_ctx
    ext_ref[0:CONV_PAD, :] = jnp.zeros((CONV_PAD, ext_ref.shape[1]), F32)
    ext_ref[off:CONV_PAD, :] = ctx_ref[...]
    if seq % 8:
        ext_ref[CONV_PAD:, :] = jnp.zeros((ext_ref.shape[0] - CONV_PAD, ext_ref.shape[1]), F32)
    ext_ref[CONV_PAD:CONV_PAD + seq, :] = u_ref[...]
    w = w_ref[...]
    out_rows = min(rows, seq)

    def chunk(i, _):
        base = pl.multiple_of(i * rows, rows)
        span = rows + CONV_PAD
        parts = []
        for c0 in range(0, ext_ref.shape[1], LANES):
            win = ext_ref[pl.ds(base, span), c0:c0 + LANES]
            part = jnp.zeros((rows, LANES), F32) + b_ref[:, c0:c0 + LANES]
            for s in range(8):
                shifted = win if s == 0 else pltpu.roll(win, span - s, axis=0)
                for k in range(taps):
                    o = off + k - s
                    if o % 8 == 0:
                        part = part + w[k:k + 1, c0:c0 + LANES] * shifted[o:o + rows]
            parts.append(part)
        acc = jnp.concatenate(parts, axis=1)
        mu = jnp.mean(acc, axis=-1, keepdims=True)
        xc = acc - mu
        y = xc * lax.rsqrt(jnp.mean(xc * xc, axis=-1, keepdims=True) + EPS) * g_ref[...] + nb_ref[...]
        y = y * _sigmoid(y)
        o_ref[pl.ds(base, out_rows), :] = y[:out_rows]
        return 0

    lax.fori_loop(0, max(seq // rows, 1), chunk, 0)


def _conv_module(ctx, u, w, b, g, nb):
    bsz, seq, ch = u.shape
    taps = w.shape[0]
    rows = 32 if seq >= 32 else 8
    ext_rows = CONV_PAD + max(seq, rows)
    vec = lambda a: pl.BlockSpec(a.shape, lambda i: (0, 0))
    return pl.pallas_call(
        functools.partial(_conv_kernel, taps=taps, seq=seq, rows=rows),
        grid=(bsz,),
        in_specs=[pl.BlockSpec((None, taps - 1, ch), lambda i: (i, 0, 0)),
                  pl.BlockSpec((None, seq, ch), lambda i: (i, 0, 0)),
                  vec(w), vec(b), vec(g), vec(nb)],
        out_specs=pl.BlockSpec((None, seq, ch), lambda i: (i, 0, 0)),
        out_shape=jax.ShapeDtypeStruct((bsz, seq, ch), F32),
        scratch_shapes=[pltpu.VMEM((ext_rows, ch), F32)],
        compiler_params=_params("parallel"),
        name="conv_module",
    )(ctx, u, w, b, g, nb)


def _outproj_kernel(x_ref, att_ref, conv_ref, wa_ref, wc_ref, o_ref):
    o_ref[...] = (x_ref[...] + _dot(att_ref[...].astype(BF16), wa_ref[...])
                  + _dot(conv_ref[...].astype(BF16), wc_ref[...]))


def _outproj(x, att, conv, wa, wc):
    t, d = x.shape
    tm = _tile(t, 512)
    row = lambda a: pl.BlockSpec((tm, a.shape[1]), lambda i: (i, 0))
    full = lambda a: pl.BlockSpec(a.shape, lambda i: (0, 0))
    return pl.pallas_call(
        _outproj_kernel,
        grid=(t // tm,),
        in_specs=[row(x), row(att), row(conv), full(wa), full(wc)],
        out_specs=row(x),
        out_shape=jax.ShapeDtypeStruct((t, d), F32),
        compiler_params=_params("parallel"),
        name="outproj",
    )(x, att, conv, wa, wc)


def _swiglu_hidden(hn, wg_ref, wu_ref, h_ref):
    d_ff = h_ref.shape[1]
    for c0 in range(0, d_ff, FF_CHUNK):
        g = _dot(hn, wg_ref[:, c0:c0 + FF_CHUNK])
        u = _dot(hn, wu_ref[:, c0:c0 + FF_CHUNK])
        h_ref[:, c0:c0 + FF_CHUNK] = (g * _sigmoid(g) * u).astype(BF16)


def _ffn_kernel(x_ref, g_ref, wg_ref, wu_ref, wd_ref, o_ref, h_ref):
    x = x_ref[...]
    _swiglu_hidden(_rms(x, g_ref[...]).astype(BF16), wg_ref, wu_ref, h_ref)
    o_ref[...] = x + _dot(h_ref[...], wd_ref[...])


def _resident(shape, index):
    return pl.BlockSpec(shape, index, pipeline_mode=pl.Buffered(1))


def _ffn(x, g, wg, wu, wd):
    t, d = x.shape
    d_ff = wg.shape[1]
    tm = _tile(t, 512)
    row = pl.BlockSpec((tm, d), lambda i: (i, 0))
    return pl.pallas_call(
        _ffn_kernel,
        grid=(t // tm,),
        in_specs=[row, pl.BlockSpec(g.shape, lambda i: (0, 0)),
                  _resident(wg.shape, lambda i: (0, 0)), _resident(wu.shape, lambda i: (0, 0)),
                  _resident(wd.shape, lambda i: (0, 0))],
        out_specs=row,
        out_shape=jax.ShapeDtypeStruct((t, d), F32),
        scratch_shapes=[pltpu.VMEM((tm, d_ff), BF16)],
        compiler_params=_params("parallel"),
        name="ffn_dense",
    )(x, g, wg, wu, wd)


def _expert_kernel(be_ref, x_ref, wg_ref, wu_ref, wd_ref, o_ref, h_ref):
    _swiglu_hidden(x_ref[...], wg_ref, wu_ref, h_ref)
    o_ref[...] = _dot(h_ref[...], wd_ref[...])


def _expert_blocks(xg, block_e, wg, wu, wd):
    rows, d = xg.shape
    d_ff = wg.shape[2]
    grid_spec = pltpu.PrefetchScalarGridSpec(
        num_scalar_prefetch=1,
        grid=(rows // MOE_ROWS,),
        in_specs=[pl.BlockSpec((MOE_ROWS, d), lambda i, be: (i, 0)),
                  _resident((None, d, d_ff), lambda i, be: (be[i], 0, 0)),
                  _resident((None, d, d_ff), lambda i, be: (be[i], 0, 0)),
                  _resident((None, d_ff, d), lambda i, be: (be[i], 0, 0))],
        out_specs=pl.BlockSpec((MOE_ROWS, d), lambda i, be: (i, 0)),
        scratch_shapes=[pltpu.VMEM((MOE_ROWS, d_ff), BF16)],
    )
    return pl.pallas_call(
        _expert_kernel,
        grid_spec=grid_spec,
        out_shape=jax.ShapeDtypeStruct((rows, d), F32),
        compiler_params=_params("arbitrary"),
        name="moe_experts",
    )(block_e, xg, wg, wu, wd)


def _router_kernel(x_ref, g_ref, r_ref, hn_ref, route_ref, *, n_experts):
    hn = _rms(x_ref[...], g_ref[...])
    hn_ref[...] = hn.astype(BF16)
    h_hi, h_lo = _split_bf16(hn)
    r_hi, r_lo = _split_bf16(r_ref[...])
    logits = _dot(h_hi, r_hi) + (_dot(h_hi, r_lo) + _dot(h_lo, r_hi))
    lane = lax.broadcasted_iota(jnp.int32, logits.shape, 1).astype(F32)
    neg = jnp.float32(-jnp.inf)
    logits = jnp.where(lane < n_experts, logits, neg)
    v1 = jnp.max(logits, axis=-1, keepdims=True)
    i1 = jnp.min(jnp.where(logits == v1, lane, float(LANES)), axis=-1, keepdims=True)
    rest = jnp.where(lane == i1, neg, logits)
    v2 = jnp.max(rest, axis=-1, keepdims=True)
    i2 = jnp.min(jnp.where(rest == v2, lane, float(LANES)), axis=-1, keepdims=True)
    e = jnp.exp(v2 - v1)
    g1 = 1.0 / (1.0 + e)
    g2 = e / (1.0 + e)
    route_ref[...] = jnp.where(lane == 0, i1, jnp.where(lane == 1, i2, jnp.where(lane == 2, g1,
                               jnp.where(lane == 3, g2, 0.0))))


def _router(x, g, router_pad, n_experts):
    t, d = x.shape
    tm = _tile(t, 512)
    row = lambda n: pl.BlockSpec((tm, n), lambda i: (i, 0))
    return pl.pallas_call(
        functools.partial(_router_kernel, n_experts=n_experts),
        grid=(t // tm,),
        in_specs=[row(d), pl.BlockSpec(g.shape, lambda i: (0, 0)),
                  pl.BlockSpec(router_pad.shape, lambda i: (0, 0))],
        out_specs=[row(d), row(LANES)],
        out_shape=[jax.ShapeDtypeStruct((t, d), BF16), jax.ShapeDtypeStruct((t, LANES), F32)],
        compiler_params=_params("parallel"),
        name="router",
    )(x, g, router_pad)


def _moe(x, g, router, wg, wu, wd):
    t, d = x.shape
    n_experts = router.shape[1]
    router_pad = jnp.pad(router, ((0, 0), (0, LANES - n_experts)))
    hn, route = _router(x, g, router_pad, n_experts)
    flat_e = route[:, :TOP_K].astype(jnp.int32).reshape(-1)
    n_assign = t * TOP_K
    onehot = (flat_e[:, None] == jnp.arange(n_experts, dtype=jnp.int32)[None, :]).astype(jnp.int32)
    before = jnp.cumsum(onehot, axis=0) - onehot
    counts = jnp.sum(onehot, axis=0)
    padded = (counts + MOE_ROWS - 1) // MOE_ROWS * MOE_ROWS
    pad_end = jnp.cumsum(padded)
    pad_start = pad_end - padded
    dest = jnp.sum(onehot * (before + pad_start[None, :]), axis=1)
    n_blocks = -(-n_assign // MOE_ROWS) + n_experts
    rows = n_blocks * MOE_ROWS
    row_tok = jnp.zeros((rows,), jnp.int32).at[dest].set(jnp.arange(n_assign, dtype=jnp.int32) // TOP_K)
    block_e = jnp.minimum(jnp.searchsorted(pad_end, jnp.arange(n_blocks, dtype=jnp.int32) * MOE_ROWS,
                                           side='right'), n_experts - 1).astype(jnp.int32)
    y = _expert_blocks(hn[row_tok], block_e, wg, wu, wd)
    pos = dest.reshape(t, TOP_K)
    return y[pos[:, 0]], y[pos[:, 1]], route


def _ple_kernel(*refs, n_add, final):
    x_ref = refs[0]
    p_ref, g_ref, wgate_ref, wproj_ref = refs[1 + n_add:5 + n_add]
    fg_ref = refs[5 + n_add] if final else None
    o_ref = refs[-1]
    x = x_ref[...]
    if n_add:
        ya_ref, yb_ref, route_ref = refs[1:4]
        route = route_ref[...]
        x = x + (ya_ref[...] * route[:, TOP_K:TOP_K + 1] + yb_ref[...] * route[:, TOP_K + 1:TOP_K + 2])
    gate = _sigmoid(_dot(_rms(x, g_ref[...]).astype(BF16), wgate_ref[...]))
    x = x + gate * _dot(p_ref[...].astype(BF16), wproj_ref[...])
    o_ref[...] = _rms(x, fg_ref[...]) if final else x


def _ple(x, adds, p, g, wgate, wproj, final_g):
    t, d = x.shape
    tm = _tile(t, 512)
    row = lambda a: pl.BlockSpec((tm, a.shape[1]), lambda i: (i, 0))
    full = lambda a: pl.BlockSpec(a.shape, lambda i: (0, 0))
    final = final_g is not None
    ops = [x, *adds, p, g, wgate, wproj] + ([final_g] if final else [])
    specs = [row(x)] + [row(a) for a in adds] + [row(p), full(g), full(wgate), full(wproj)]
    specs += [full(final_g)] if final else []
    return pl.pallas_call(
        functools.partial(_ple_kernel, n_add=len(adds), final=final),
        grid=(t // tm,),
        in_specs=specs,
        out_specs=row(x),
        out_shape=jax.ShapeDtypeStruct((t, d), F32),
        compiler_params=_params("parallel"),
        name="ple",
    )(*ops)


def kernel(x_prompt, x_sample, cache_k, cache_v, state_conv, page_table, p_prompt, p_sample,
           w_in, sb_bias, w_out, conv_w, conv_b, conv_norm_g, conv_norm_b, norm_mix_g, norm_ffn_g,
           norm_ple_g, w_ple_gate, w_ple_proj, ffn_w_gate, ffn_w_up, ffn_w_down,
           moe_router, moe_w_gate, moe_w_up, moe_w_down, final_norm_g):
    depth, d_model = norm_mix_g.shape
    n_heads = sb_bias.shape[1]
    att = n_heads * HEAD_DIM
    conv = conv_w.shape[2]
    taps = conv_w.shape[1]
    n_pool, page = cache_k.shape[1], cache_k.shape[2]
    cache_kt = jnp.transpose(cache_k, (0, 1, 3, 4, 2)).reshape(depth, n_pool, att, page)
    cache_vt = jnp.transpose(cache_v, (0, 1, 3, 4, 2)).reshape(depth, n_pool, att, page)
    vec = lambda a: a.reshape(1, -1)
    final_g = vec(final_norm_g)
    bsz, seq, _ = x_prompt.shape

    def mix(i, x, a, u3, ctx, p3):
        b, s, _ = u3.shape
        c3 = _conv_module(ctx, u3, conv_w[i], vec(conv_b[i]), vec(conv_norm_g[i]), vec(conv_norm_b[i]))
        wo = w_out[i].astype(BF16)
        x = _outproj(x, a, c3.reshape(b * s, conv), wo[:att], wo[att:])
        j = i // 2
        if i % 2 == 0:
            x = _ffn(x, vec(norm_ffn_g[i]), ffn_w_gate[j].astype(BF16), ffn_w_up[j].astype(BF16),
                     ffn_w_down[j].astype(BF16))
            adds = []
        else:
            adds = list(_moe(x, vec(norm_ffn_g[i]), moe_router[j], moe_w_gate[j].astype(BF16),
                             moe_w_up[j].astype(BF16), moe_w_down[j].astype(BF16)))
        x = _ple(x, adds, p3.reshape(b * s, -1), vec(norm_ple_g[i]), w_ple_gate[i].astype(BF16),
                 w_ple_proj[i].astype(BF16), final_g if i == depth - 1 else None)
        return x, jnp.concatenate([ctx, u3], axis=1)[:, -(taps - 1):]

    xp = x_prompt.reshape(bsz * seq, d_model)
    db, n_q, _ = x_sample.shape
    xs = x_sample.reshape(db * n_q, d_model)
    conv_zero = jnp.zeros((bsz, taps - 1, conv), F32)
    kv_t = None
    cp_l, ks_l, vs_l, cs_l = [], [], [], []
    for i in range(depth):
        g = vec(norm_mix_g[i])
        w = w_in[i].astype(BF16)
        wkv_t = w_in[i][:, att:3 * att].T.astype(BF16)
        q, k_t, v_t, u = _inproj_kv_t(xp, g, w, wkv_t, att, conv, i, depth, bsz, kv_t)
        kv_t = (k_t, v_t)
        a = _sb_prompt(q.reshape(bsz, seq, att), k_t, v_t, i, sb_bias[i]).reshape(bsz * seq, att)
        xp, cp = mix(i, xp, a, u.reshape(bsz, seq, conv), conv_zero, p_prompt[i])

        q, k, v, u = _inproj(xs, g, w, att, conv)
        q3, k3, v3 = (t.reshape(db, n_q, att) for t in (q, k, v))
        a = _sb_paged(q3, k3, v3, cache_kt, cache_vt, i, page_table, sb_bias[i]).reshape(db * n_q, att)
        xs, cs = mix(i, xs, a, u.reshape(db, n_q, conv), state_conv[i], p_sample[i])
        cp_l.append(cp)
        ks_l.append(k3.reshape(db, n_q, n_heads, HEAD_DIM))
        vs_l.append(v3.reshape(db, n_q, n_heads, HEAD_DIM))
        cs_l.append(cs)

    heads_last = lambda t: jnp.transpose(t.reshape(depth, bsz, n_heads, HEAD_DIM, seq), (0, 1, 4, 2, 3))
    return (xp.reshape(bsz, seq, d_model), xs.reshape(db, n_q, d_model), heads_last(kv_t[0]), heads_last(kv_t[1]),
            jnp.stack(cp_l), jnp.stack(ks_l), jnp.stack(vs_l), jnp.stack(cs_l))
```

```python
import functools

import jax
import jax.numpy as jnp
from jax import lax
from jax.experimental import pallas as pl
from jax.experimental.pallas import tpu as pltpu

F32 = jnp.float32
BF16 = jnp.bfloat16
MX = jnp.bfloat16
EPS = 1e-6
HEAD_DIM = 64
LANES = 128
TOP_K = 2
VMEM_LIMIT = 56 * 1024 * 1024
PAGES_PER_STEP = 8
MOE_ROWS = 256
FF_CHUNK = 256


def _params(*sem):
    return pltpu.CompilerParams(dimension_semantics=sem, vmem_limit_bytes=VMEM_LIMIT)


def _tile(n, pref):
    if n <= pref:
        return n
    t = pref - pref % 8
    while n % t:
        t -= 8
    return t


def _dot(a, b):
    return jnp.dot(a, b, preferred_element_type=F32)


def _dot_nt(a, b):
    return lax.dot_general(a, b, (((1,), (1,)), ((), ())), preferred_element_type=F32)


def _rms(xf, g):
    return xf * lax.rsqrt(jnp.mean(xf * xf, axis=-1, keepdims=True) + EPS) * g


def _sigmoid(x):
    return 1.0 / (1.0 + jnp.exp(-x))


def _split_bf16(x):
    hi = x.astype(BF16)
    lo = (x - hi.astype(F32)).astype(BF16)
    return hi, lo


def _inproj_kernel(*refs, att, conv, kv_t):
    x_ref, g_ref, w_ref = refs[:3]
    q_ref, k_ref, v_ref, u_ref = refs[-4:]
    hn = _rms(x_ref[...], g_ref[...]).astype(BF16)
    col = lambda c0, n: _dot(hn, w_ref[:, c0:c0 + n])
    q_ref[...] = col(0, att)
    if kv_t:
        kv = _dot_nt(refs[3][...], hn)
        k_ref[...] = kv[:att]
        v_ref[...] = kv[att:]
    else:
        k_ref[...] = col(att, att)
        v_ref[...] = col(2 * att, att)
    a = col(3 * att, conv)
    g = col(3 * att + conv, conv)
    u_ref[...] = a * _sigmoid(g)


def _inproj(x, g, w, att, conv):
    t, d = x.shape
    tm = _tile(t, 512)
    row = lambda n: pl.BlockSpec((tm, n), lambda i: (i, 0))
    full = lambda a: pl.BlockSpec(a.shape, lambda i: (0, 0))
    return pl.pallas_call(
        functools.partial(_inproj_kernel, att=att, conv=conv, kv_t=False),
        grid=(t // tm,),
        in_specs=[row(d), full(g), full(w)],
        out_specs=[row(att), row(att), row(att), row(conv)],
        out_shape=[jax.ShapeDtypeStruct((t, att), F32)] * 3 + [jax.ShapeDtypeStruct((t, conv), F32)],
        compiler_params=_params("parallel"),
        name="inproj",
    )(x, g, w)


def _inproj_kv_t(x, g, w, wkv_t, att, conv, layer, depth, batch, kv_prev):
    t, d = x.shape
    seq = t // batch
    tm = _tile(seq, 512)
    per_seq = seq // tm
    row = lambda n: pl.BlockSpec((tm, n), lambda i: (i, 0))
    full = lambda a: pl.BlockSpec(a.shape, lambda i: (0, 0))
    slab = pl.BlockSpec((None, None, att, tm), lambda i: (layer, i // per_seq, 0, i % per_seq))
    kv_shape = jax.ShapeDtypeStruct((depth, batch, att, seq), F32)
    prev = list(kv_prev) if kv_prev is not None else []
    return pl.pallas_call(
        functools.partial(_inproj_kernel, att=att, conv=conv, kv_t=True),
        grid=(t // tm,),
        in_specs=[row(d), full(g), full(w), full(wkv_t)] + [pl.BlockSpec(memory_space=pl.ANY)] * len(prev),
        out_specs=[row(att), slab, slab, row(conv)],
        out_shape=[jax.ShapeDtypeStruct((t, att), F32), kv_shape, kv_shape, jax.ShapeDtypeStruct((t, conv), F32)],
        input_output_aliases={4: 1, 5: 2} if prev else {},
        compiler_params=_params("parallel"),
        name="inproj_kv_t",
    )(x, g, w, wkv_t, *prev)


def _log_sigmoids(z):
    ls = jnp.minimum(z, 0.0) - jnp.log(1.0 + jnp.exp(-jnp.abs(z)))
    return ls, ls - z


def _later_than(n):
    j = lax.broadcasted_iota(jnp.int32, (n, n), 0)
    s = lax.broadcasted_iota(jnp.int32, (n, n), 1)
    return jnp.where(j > s, 1.0, 0.0).astype(MX)


def _sb_weights(z, tri, carry, mask):
    ls, lk = _log_sigmoids(z)
    if mask is not None:
        lk = jnp.where(mask, lk, 0.0)
    later = _dot(lk.astype(MX), tri)
    a = jnp.exp(ls + later + carry)
    if mask is not None:
        a = jnp.where(mask, a, 0.0)
    return a, carry + later[:, 0:1] + lk[:, 0:1]


def _sb_prompt_kernel(bias_ref, q_ref, k_ref, v_ref, o_ref, z_ref, a_ref, carry_ref, *, tq):
    hp = pl.program_id(1)
    qi = pl.program_id(2)
    first = lax.broadcasted_iota(jnp.int32, (tq, LANES), 1) < HEAD_DIM
    q = q_ref[...] * (HEAD_DIM ** -0.5)
    q2 = jnp.concatenate([jnp.where(first, q, 0.0), jnp.where(first, 0.0, q)], axis=0).astype(MX)
    bias_a = bias_ref[2 * hp]
    bias_b = bias_ref[2 * hp + 1]
    tri = _later_than(tq)
    r = lax.broadcasted_iota(jnp.int32, (2 * tq, tq), 0)
    c = lax.broadcasted_iota(jnp.int32, (2 * tq, tq), 1)
    causal = c < jnp.where(r < tq, r, r - tq)
    v_first = lax.broadcasted_iota(jnp.int32, (LANES, tq), 0) < HEAD_DIM

    def cols(j):
        return pl.ds(pl.multiple_of(j * tq, tq), tq)

    def logits(j):
        z = _dot(q2, k_ref[:, cols(j)].astype(MX))
        return jnp.concatenate([z[:tq] + bias_a, z[tq:] + bias_b], axis=0)

    def weighted_values(a2, j):
        vb = v_ref[:, cols(j)]
        v2 = jnp.concatenate([jnp.where(v_first, vb, 0.0), jnp.where(v_first, 0.0, vb)], axis=1).astype(MX)
        return _dot_nt(a2, v2)

    def keep(a, carry):
        a_ref[...] = jnp.concatenate([a[:tq], a[tq:]], axis=1).astype(MX)
        carry_ref[...] = carry

    keep(*_sb_weights(logits(qi), tri, jnp.zeros((2 * tq, 1), F32), causal))
    z_ref[...] = logits(jnp.maximum(qi - 1, 0))
    o_ref[...] = jnp.zeros((tq, LANES), F32)

    def body(n, c):
        j = qi - 1 - n
        o_ref[...] += weighted_values(a_ref[...], j + 1)
        z = z_ref[...]
        z_ref[...] = logits(jnp.maximum(j - 1, 0))
        keep(*_sb_weights(z, tri, carry_ref[...], None))
        return c

    lax.fori_loop(0, qi, body, 0)
    o_ref[...] += weighted_values(a_ref[...], 0)


def _sb_prompt(q, k_t, v_t, layer, bias):
    b, s, w = q.shape
    tq = _tile(s, 256)
    qspec = pl.BlockSpec((None, tq, LANES), lambda bi, hp, qi: (bi, qi, hp))
    kvspec = pl.BlockSpec((None, None, LANES, s), lambda bi, hp, qi: (layer, bi, hp, 0))
    return pl.pallas_call(
        functools.partial(_sb_prompt_kernel, tq=tq),
        grid=(b, w // LANES, s // tq),
        in_specs=[pl.BlockSpec(memory_space=pltpu.SMEM), qspec, kvspec, kvspec],
        out_specs=qspec,
        out_shape=jax.ShapeDtypeStruct((b, s, w), F32),
        scratch_shapes=[pltpu.VMEM((2 * tq, tq), F32), pltpu.VMEM((tq, 2 * tq), MX),
                        pltpu.VMEM((2 * tq, 1), F32)],
        compiler_params=_params("parallel", "parallel", "arbitrary"),
        name="sb_prompt",
    )(bias, q, k_t, v_t)


def _sb_paged_kernel(pt_ref, bias_ref, q_ref, kn_ref, vn_ref, *rest, n_heads, n_q, page):
    kpages = rest[:PAGES_PER_STEP]
    vpages = rest[PAGES_PER_STEP:2 * PAGES_PER_STEP]
    o_ref, qbd_ref, acc_ref, new_ref, carry_ref, w_ref = rest[2 * PAGES_PER_STEP:]
    step = pl.program_id(1)
    rows = n_q * n_heads
    width = n_heads * HEAD_DIM
    pairs = PAGES_PER_STEP // 2
    bias = bias_ref[...]
    tri = _later_than(2 * page)

    @pl.when(step == 0)
    def _init():
        q = q_ref[...] * (HEAD_DIM ** -0.5)
        rep = jnp.concatenate([jnp.broadcast_to(q[t:t + 1, :], (n_heads, width)) for t in range(n_q)], axis=0)
        row = lax.broadcasted_iota(jnp.int32, (rows, width), 0)
        lane = lax.broadcasted_iota(jnp.int32, (rows, width), 1)
        qbd = jnp.where(lane // HEAD_DIM == row % n_heads, rep, 0.0)
        qbd_ref[...] = qbd
        pad = jnp.zeros((page - n_q, width), F32)
        kn = jnp.concatenate([kn_ref[...], pad], axis=0)
        vn = jnp.concatenate([vn_ref[...], pad], axis=0)
        key = lax.broadcasted_iota(jnp.int32, (rows, page), 1)
        t = lax.broadcasted_iota(jnp.int32, (rows, page), 0) // n_heads
        z = _dot_nt(qbd.astype(MX), kn.astype(MX)) + bias
        a, carry = _sb_weights(z, tri[:page, :page], jnp.zeros((rows, 1), F32), key < t)
        new_ref[...] = _dot(a.astype(MX), vn.astype(MX))
        carry_ref[...] = carry
        acc_ref[...] = jnp.zeros(acc_ref.shape, F32)
        w_ref[...] = jnp.zeros(w_ref.shape, MX)

    w_prev = w_ref[...]
    qbd = qbd_ref[...].astype(MX)
    scores = []
    for i in range(pairs):
        keys = jnp.concatenate([kpages[2 * i][...], kpages[2 * i + 1][...]], axis=1)
        scores.append(_log_sigmoids(_dot(qbd, keys.astype(MX)) + bias))
    later = _dot(jnp.concatenate([lk for _, lk in scores], axis=0).astype(MX), tri)
    carry = carry_ref[...]
    weights = [None] * pairs
    for i in reversed(range(pairs)):
        ls, lk = scores[i]
        lt = later[i * rows:(i + 1) * rows]
        weights[i] = jnp.exp(ls + lt + carry)
        carry = carry + lt[:, 0:1] + lk[:, 0:1]
    carry_ref[...] = carry
    w_ref[...] = jnp.concatenate(weights, axis=1).astype(MX)
    values = jnp.concatenate([vpages[p][...] for p in range(PAGES_PER_STEP)], axis=1)
    acc_ref[...] += _dot_nt(values.astype(MX), w_prev)

    @pl.when(step == pl.num_programs(1) - 1)
    def _finish():
        r = lax.broadcasted_iota(jnp.int32, (width, rows), 0)
        c = lax.broadcasted_iota(jnp.int32, (width, rows), 1)
        hi, lo = _split_bf16(jnp.where(c % n_heads == r // HEAD_DIM, acc_ref[...], 0.0))
        t = lax.broadcasted_iota(jnp.int32, (8, rows), 0)
        cc = lax.broadcasted_iota(jnp.int32, (8, rows), 1)
        pick = jnp.where(cc // n_heads == t, 1.0, 0.0).astype(BF16)
        past = _dot_nt(pick, hi) + _dot_nt(pick, lo)
        row = lax.broadcasted_iota(jnp.int32, (rows, width), 0)
        lane = lax.broadcasted_iota(jnp.int32, (rows, width), 1)
        own = jnp.where(lane // HEAD_DIM == row % n_heads, new_ref[...], 0.0)
        o_ref[...] = past[:n_q] + jnp.sum(own.reshape(n_q, n_heads, width), axis=1)


def _sb_paged(q, k_new, v_new, cache_kt, cache_vt, layer, page_table, bias):
    db, n_q, width = q.shape
    n_heads = width // HEAD_DIM
    page = cache_kt.shape[3]
    n_pages = page_table.shape[1]
    steps = n_pages // PAGES_PER_STEP
    rows = n_q * n_heads
    bias_col = jnp.tile(bias, n_q).reshape(rows, 1)

    def page_spec(p, delay):
        def index(bi, si, pt):
            group = steps - 1 - jnp.clip(si - delay, 0, steps - 1)
            return (layer, pt[bi, group * PAGES_PER_STEP + p], 0, 0)
        return pl.BlockSpec((None, None, width, page), index)

    small = pl.BlockSpec((None, n_q, width), lambda bi, si, pt: (bi, 0, 0))
    kpages = [page_spec(p, 0) for p in range(PAGES_PER_STEP)]
    vpages = [page_spec(p, 1) for p in range(PAGES_PER_STEP)]
    grid_spec = pltpu.PrefetchScalarGridSpec(
        num_scalar_prefetch=1,
        grid=(db, steps + 1),
        in_specs=[pl.BlockSpec((rows, 1), lambda bi, si, pt: (0, 0)), small, small, small] + kpages + vpages,
        out_specs=small,
        scratch_shapes=[pltpu.VMEM((rows, width), F32), pltpu.VMEM((width, rows), F32),
                        pltpu.VMEM((rows, width), F32), pltpu.VMEM((rows, 1), F32),
                        pltpu.VMEM((rows, PAGES_PER_STEP * page), MX)],
    )
    return pl.pallas_call(
        functools.partial(_sb_paged_kernel, n_heads=n_heads, n_q=n_q, page=page),
        grid_spec=grid_spec,
        out_shape=jax.ShapeDtypeStruct((db, n_q, width), F32),
        compiler_params=_params("parallel", "arbitrary"),
        name="sb_paged",
    )(page_table, bias_col, q, k_new, v_new, *([cache_kt] * PAGES_PER_STEP), *([cache_vt] * PAGES_PER_STEP))


CONV_PAD = 32


def _conv_kernel(ctx_ref, u_ref, w_ref, b_ref, g_ref, nb_ref, o_ref, ext_ref, *, taps, seq, rows):
    n_ctx = taps - 1
    off = CONV_PAD - n_ctx
    ext_ref[0:CONV_PAD, :] = jnp.zeros((CONV_PAD, ext_ref.shape[1]), F32)
    ext_ref[off:CONV_PAD, :] = ctx_ref[...]
    if seq % 8:
        ext_ref[CONV_PAD:, :] = jnp.zeros((ext_ref.shape[0] - CONV_PAD, ext_ref.shape[1]), F32)
    ext_ref[CONV_PAD:CONV_PAD + seq, :] = u_ref[...]
    w = w_ref[...]
    out_rows = min(rows, seq)

    def chunk(i, _):
        base = pl.multiple_of(i * rows, rows)
        span = rows + CONV_PAD
        parts = []
        for c0 in range(0, ext_ref.shape[1], LANES):
            win = ext_ref[pl.ds(base, span), c0:c0 + LANES]
            part = jnp.zeros((rows, LANES), F32) + b_ref[:, c0:c0 + LANES]
            for s in range(8):
                shifted = win if s == 0 else pltpu.roll(win, span - s, axis=0)
                for k in range(taps):
                    o = off + k - s
                    if o % 8 == 0:
                        part = part + w[k:k + 1, c0:c0 + LANES] * shifted[o:o + rows]
            parts.append(part)
        acc = jnp.concatenate(parts, axis=1)
        mu = jnp.mean(acc, axis=-1, keepdims=True)
        xc = acc - mu
        y = xc * lax.rsqrt(jnp.mean(xc * xc, axis=-1, keepdims=True) + EPS) * g_ref[...] + nb_ref[...]
        y = y * _sigmoid(y)
        o_ref[pl.ds(base, out_rows), :] = y[:out_rows]
        return 0

    lax.fori_loop(0, max(seq // rows, 1), chunk, 0)


def _conv_module(ctx, u, w, b, g, nb):
    bsz, seq, ch = u.shape
    taps = w.shape[0]
    rows = 128 if seq >= 128 else 8
    ext_rows = CONV_PAD + max(seq, rows)
    vec = lambda a: pl.BlockSpec(a.shape, lambda i: (0, 0))
    return pl.pallas_call(
        functools.partial(_conv_kernel, taps=taps, seq=seq, rows=rows),
        grid=(bsz,),
        in_specs=[pl.BlockSpec((None, taps - 1, ch), lambda i: (i, 0, 0)),
                  pl.BlockSpec((None, seq, ch), lambda i: (i, 0, 0)),
                  vec(w), vec(b), vec(g), vec(nb)],
        out_specs=pl.BlockSpec((None, seq, ch), lambda i: (i, 0, 0)),
        out_shape=jax.ShapeDtypeStruct((bsz, seq, ch), F32),
        scratch_shapes=[pltpu.VMEM((ext_rows, ch), F32)],
        compiler_params=_params("parallel"),
        name="conv_module",
    )(ctx, u, w, b, g, nb)


def _outproj_kernel(x_ref, att_ref, conv_ref, wa_ref, wc_ref, o_ref):
    o_ref[...] = (x_ref[...] + _dot(att_ref[...].astype(BF16), wa_ref[...])
                  + _dot(conv_ref[...].astype(BF16), wc_ref[...]))


def _outproj(x, att, conv, wa, wc):
    t, d = x.shape
    tm = _tile(t, 512)
    row = lambda a: pl.BlockSpec((tm, a.shape[1]), lambda i: (i, 0))
    full = lambda a: pl.BlockSpec(a.shape, lambda i: (0, 0))
    return pl.pallas_call(
        _outproj_kernel,
        grid=(t // tm,),
        in_specs=[row(x), row(att), row(conv), full(wa), full(wc)],
        out_specs=row(x),
        out_shape=jax.ShapeDtypeStruct((t, d), F32),
        compiler_params=_params("parallel"),
        name="outproj",
    )(x, att, conv, wa, wc)


def _swiglu_hidden(hn, wg_ref, wu_ref, h_ref):
    d_ff = h_ref.shape[1]
    for c0 in range(0, d_ff, FF_CHUNK):
        g = _dot(hn, wg_ref[:, c0:c0 + FF_CHUNK].astype(BF16))
        u = _dot(hn, wu_ref[:, c0:c0 + FF_CHUNK].astype(BF16))
        h_ref[:, c0:c0 + FF_CHUNK] = (g * _sigmoid(g) * u).astype(BF16)


def _ffn_kernel(x_ref, g_ref, wg_ref, wu_ref, wd_ref, o_ref, h_ref):
    x = x_ref[...]
    _swiglu_hidden(_rms(x, g_ref[...]).astype(BF16), wg_ref, wu_ref, h_ref)
    o_ref[...] = x + _dot(h_ref[...], wd_ref[...])


def _resident(shape, index):
    return pl.BlockSpec(shape, index, pipeline_mode=pl.Buffered(1))


def _ffn(x, g, wg, wu, wd):
    t, d = x.shape
    d_ff = wg.shape[1]
    tm = _tile(t, 512)
    row = pl.BlockSpec((tm, d), lambda i: (i, 0))
    return pl.pallas_call(
        _ffn_kernel,
        grid=(t // tm,),
        in_specs=[row, pl.BlockSpec(g.shape, lambda i: (0, 0)),
                  _resident(wg.shape, lambda i: (0, 0)), _resident(wu.shape, lambda i: (0, 0)),
                  _resident(wd.shape, lambda i: (0, 0))],
        out_specs=row,
        out_shape=jax.ShapeDtypeStruct((t, d), F32),
        scratch_shapes=[pltpu.VMEM((tm, d_ff), BF16)],
        compiler_params=_params("parallel"),
        name="ffn_dense",
    )(x, g, wg, wu, wd)


def _expert_kernel(be_ref, x_ref, wg_ref, wu_ref, wd_ref, o_ref, h_ref):
    _swiglu_hidden(x_ref[...], wg_ref, wu_ref, h_ref)
    o_ref[...] = _dot(h_ref[...], wd_ref[...].astype(BF16))


def _expert_blocks(xg, block_e, wg, wu, wd):
    rows, d = xg.shape
    d_ff = wg.shape[2]
    grid_spec = pltpu.PrefetchScalarGridSpec(
        num_scalar_prefetch=1,
        grid=(rows // MOE_ROWS,),
        in_specs=[pl.BlockSpec((MOE_ROWS, d), lambda i, be: (i, 0)),
                  _resident((None, d, d_ff), lambda i, be: (be[i], 0, 0)),
                  _resident((None, d, d_ff), lambda i, be: (be[i], 0, 0)),
                  _resident((None, d_ff, d), lambda i, be: (be[i], 0, 0))],
        out_specs=pl.BlockSpec((MOE_ROWS, d), lambda i, be: (i, 0)),
        scratch_shapes=[pltpu.VMEM((MOE_ROWS, d_ff), BF16)],
    )
    return pl.pallas_call(
        _expert_kernel,
        grid_spec=grid_spec,
        out_shape=jax.ShapeDtypeStruct((rows, d), F32),
        compiler_params=_params("arbitrary"),
        name="moe_experts",
    )(block_e, xg, wg, wu, wd)


def _router_kernel(x_ref, g_ref, r_ref, hn_ref, route_ref, *, n_experts):
    hn = _rms(x_ref[...], g_ref[...])
    hn_ref[...] = hn.astype(BF16)
    h_hi, h_lo = _split_bf16(hn)
    r_hi, r_lo = _split_bf16(r_ref[...])
    logits = _dot(h_hi, r_hi) + (_dot(h_hi, r_lo) + _dot(h_lo, r_hi))
    lane = lax.broadcasted_iota(jnp.int32, logits.shape, 1).astype(F32)
    neg = jnp.float32(-jnp.inf)
    logits = jnp.where(lane < n_experts, logits, neg)
    v1 = jnp.max(logits, axis=-1, keepdims=True)
    i1 = jnp.min(jnp.where(logits == v1, lane, float(LANES)), axis=-1, keepdims=True)
    rest = jnp.where(lane == i1, neg, logits)
    v2 = jnp.max(rest, axis=-1, keepdims=True)
    i2 = jnp.min(jnp.where(rest == v2, lane, float(LANES)), axis=-1, keepdims=True)
    e = jnp.exp(v2 - v1)
    g1 = 1.0 / (1.0 + e)
    g2 = e / (1.0 + e)
    route_ref[...] = jnp.where(lane == 0, i1, jnp.where(lane == 1, i2, jnp.where(lane == 2, g1,
                               jnp.where(lane == 3, g2, 0.0))))


def _router(x, g, router_pad, n_experts):
    t, d = x.shape
    tm = _tile(t, 512)
    row = lambda n: pl.BlockSpec((tm, n), lambda i: (i, 0))
    return pl.pallas_call(
        functools.partial(_router_kernel, n_experts=n_experts),
        grid=(t // tm,),
        in_specs=[row(d), pl.BlockSpec(g.shape, lambda i: (0, 0)),
                  pl.BlockSpec(router_pad.shape, lambda i: (0, 0))],
        out_specs=[row(d), row(LANES)],
        out_shape=[jax.ShapeDtypeStruct((t, d), BF16), jax.ShapeDtypeStruct((t, LANES), F32)],
        compiler_params=_params("parallel"),
        name="router",
    )(x, g, router_pad)


def _moe(x, g, router, wg, wu, wd):
    t, d = x.shape
    n_experts = router.shape[1]
    router_pad = jnp.pad(router, ((0, 0), (0, LANES - n_experts)))
    hn, route = _router(x, g, router_pad, n_experts)
    flat_e = route[:, :TOP_K].astype(jnp.int32).reshape(-1)
    n_assign = t * TOP_K
    onehot = (flat_e[:, None] == jnp.arange(n_experts, dtype=jnp.int32)[None, :]).astype(jnp.int32)
    before = jnp.cumsum(onehot, axis=0) - onehot
    counts = jnp.sum(onehot, axis=0)
    padded = (counts + MOE_ROWS - 1) // MOE_ROWS * MOE_ROWS
    pad_end = jnp.cumsum(padded)
    pad_start = pad_end - padded
    dest = jnp.sum(onehot * (before + pad_start[None, :]), axis=1)
    n_blocks = -(-n_assign // MOE_ROWS) + n_experts
    rows = n_blocks * MOE_ROWS
    row_tok = jnp.zeros((rows,), jnp.int32).at[dest].set(
        jnp.arange(n_assign, dtype=jnp.int32) // TOP_K, unique_indices=True, mode='promise_in_bounds')
    block_start = jnp.arange(n_blocks, dtype=jnp.int32) * MOE_ROWS
    block_e = jnp.minimum(jnp.sum((pad_end[None, :] <= block_start[:, None]).astype(jnp.int32), axis=1),
                          n_experts - 1)
    y = _expert_blocks(hn[row_tok], block_e, wg, wu, wd)
    pos = dest.reshape(t, TOP_K)
    return y[pos[:, 0]], y[pos[:, 1]], route


def _ple_kernel(*refs, n_add, final):
    x_ref = refs[0]
    p_ref, g_ref, wgate_ref, wproj_ref = refs[1 + n_add:5 + n_add]
    fg_ref = refs[5 + n_add] if final else None
    o_ref = refs[-1]
    x = x_ref[...]
    if n_add:
        ya_ref, yb_ref, route_ref = refs[1:4]
        route = route_ref[...]
        x = x + (ya_ref[...] * route[:, TOP_K:TOP_K + 1] + yb_ref[...] * route[:, TOP_K + 1:TOP_K + 2])
    gate = _sigmoid(_dot(_rms(x, g_ref[...]).astype(BF16), wgate_ref[...]))
    x = x + gate * _dot(p_ref[...].astype(BF16), wproj_ref[...])
    o_ref[...] = _rms(x, fg_ref[...]) if final else x


def _ple(x, adds, p, g, wgate, wproj, final_g):
    t, d = x.shape
    tm = _tile(t, 512)
    row = lambda a: pl.BlockSpec((tm, a.shape[1]), lambda i: (i, 0))
    full = lambda a: pl.BlockSpec(a.shape, lambda i: (0, 0))
    final = final_g is not None
    ops = [x, *adds, p, g, wgate, wproj] + ([final_g] if final else [])
    specs = [row(x)] + [row(a) for a in adds] + [row(p), full(g), full(wgate), full(wproj)]
    specs += [full(final_g)] if final else []
    return pl.pallas_call(
        functools.partial(_ple_kernel, n_add=len(adds), final=final),
        grid=(t // tm,),
        in_specs=specs,
        out_specs=row(x),
        out_shape=jax.ShapeDtypeStruct((t, d), F32),
        compiler_params=_params("parallel"),
        name="ple",
    )(*ops)


def kernel(x_prompt, x_sample, cache_k, cache_v, state_conv, page_table, p_prompt, p_sample,
           w_in, sb_bias, w_out, conv_w, conv_b, conv_norm_g, conv_norm_b, norm_mix_g, norm_ffn_g,
           norm_ple_g, w_ple_gate, w_ple_proj, ffn_w_gate, ffn_w_up, ffn_w_down,
           moe_router, moe_w_gate, moe_w_up, moe_w_down, final_norm_g):
    depth, d_model = norm_mix_g.shape
    n_heads = sb_bias.shape[1]
    att = n_heads * HEAD_DIM
    conv = conv_w.shape[2]
    taps = conv_w.shape[1]
    n_pool, page = cache_k.shape[1], cache_k.shape[2]
    cache_kt = jnp.transpose(cache_k, (0, 1, 3, 4, 2)).reshape(depth, n_pool, att, page)
    cache_vt = jnp.transpose(cache_v, (0, 1, 3, 4, 2)).reshape(depth, n_pool, att, page)
    vec = lambda a: a.reshape(1, -1)
    final_g = vec(final_norm_g)
    bsz, seq, _ = x_prompt.shape

    def mix(i, x, a, u3, ctx, p3):
        b, s, _ = u3.shape
        c3 = _conv_module(ctx, u3, conv_w[i], vec(conv_b[i]), vec(conv_norm_g[i]), vec(conv_norm_b[i]))
        wo = w_out[i].astype(BF16)
        x = _outproj(x, a, c3.reshape(b * s, conv), wo[:att], wo[att:])
        j = i // 2
        if i % 2 == 0:
            x = _ffn(x, vec(norm_ffn_g[i]), ffn_w_gate[j].astype(BF16), ffn_w_up[j].astype(BF16),
                     ffn_w_down[j].astype(BF16))
            adds = []
        else:
            adds = list(_moe(x, vec(norm_ffn_g[i]), moe_router[j], moe_w_gate[j], moe_w_up[j], moe_w_down[j]))
        x = _ple(x, adds, p3.reshape(b * s, -1), vec(norm_ple_g[i]), w_ple_gate[i].astype(BF16),
                 w_ple_proj[i].astype(BF16), final_g if i == depth - 1 else None)
        return x, jnp.concatenate([ctx, u3], axis=1)[:, -(taps - 1):]

    xp = x_prompt.reshape(bsz * seq, d_model)
    db, n_q, _ = x_sample.shape
    xs = x_sample.reshape(db * n_q, d_model)
    conv_zero = jnp.zeros((bsz, taps - 1, conv), F32)
    kv_t = None
    cp_l, ks_l, vs_l, cs_l = [], [], [], []
    for i in range(depth):
        g = vec(norm_mix_g[i])
        w = w_in[i].astype(BF16)
        wkv_t = w_in[i][:, att:3 * att].T.astype(BF16)
        q, k_t, v_t, u = _inproj_kv_t(xp, g, w, wkv_t, att, conv, i, depth, bsz, kv_t)
        kv_t = (k_t, v_t)
        a = _sb_prompt(q.reshape(bsz, seq, att), k_t, v_t, i, sb_bias[i]).reshape(bsz * seq, att)
        xp, cp = mix(i, xp, a, u.reshape(bsz, seq, conv), conv_zero, p_prompt[i])

        q, k, v, u = _inproj(xs, g, w, att, conv)
        q3, k3, v3 = (t.reshape(db, n_q, att) for t in (q, k, v))
        a = _sb_paged(q3, k3, v3, cache_kt, cache_vt, i, page_table, sb_bias[i]).reshape(db * n_q, att)
        xs, cs = mix(i, xs, a, u.reshape(db, n_q, conv), state_conv[i], p_sample[i])
        cp_l.append(cp)
        ks_l.append(k3.reshape(db, n_q, n_heads, HEAD_DIM))
        vs_l.append(v3.reshape(db, n_q, n_heads, HEAD_DIM))
        cs_l.append(cs)

    heads_last = lambda t: jnp.transpose(t.reshape(depth, bsz, n_heads, HEAD_DIM, seq), (0, 1, 4, 2, 3))
    return (xp.reshape(bsz, seq, d_model), xs.reshape(db, n_q, d_model), heads_last(kv_t[0]), heads_last(kv_t[1]),
            jnp.stack(cp_l), jnp.stack(ks_l), jnp.stack(vs_l), jnp.stack(cs_l))
```

```python
import functools

import jax
import jax.numpy as jnp
from jax import lax
from jax.experimental import pallas as pl
from jax.experimental.pallas import tpu as pltpu

F32 = jnp.float32
BF16 = jnp.bfloat16
MX = jnp.bfloat16
EPS = 1e-6
HEAD_DIM = 64
LANES = 128
TOP_K = 2
VMEM_LIMIT = 56 * 1024 * 1024
PAGES_PER_STEP = 16
MOE_ROWS = 256
FF_CHUNK = 256


def _params(*sem):
    return pltpu.CompilerParams(dimension_semantics=sem, vmem_limit_bytes=VMEM_LIMIT)


def _tile(n, pref):
    if n <= pref:
        return n
    t = pref - pref % 8
    while n % t:
        t -= 8
    return t


def _dot(a, b):
    return jnp.dot(a, b, preferred_element_type=F32)


def _dot_nt(a, b):
    return lax.dot_general(a, b, (((1,), (1,)), ((), ())), preferred_element_type=F32)


def _rms(xf, g):
    return xf * lax.rsqrt(jnp.mean(xf * xf, axis=-1, keepdims=True) + EPS) * g


def _sigmoid(x):
    return 1.0 / (1.0 + jnp.exp(-x))


def _split_bf16(x):
    hi = x.astype(BF16)
    lo = (x - hi.astype(F32)).astype(BF16)
    return hi, lo


def _inproj_kernel(*refs, att, conv, kv_t):
    x_ref, g_ref, w_ref = refs[:3]
    q_ref, k_ref, v_ref, u_ref = refs[-4:]
    hn = _rms(x_ref[...], g_ref[...]).astype(BF16)
    col = lambda c0, n: _dot(hn, w_ref[:, c0:c0 + n])
    q_ref[...] = col(0, att)
    if kv_t:
        kv = _dot_nt(refs[3][...], hn)
        k_ref[...] = kv[:att]
        v_ref[...] = kv[att:]
    else:
        k_ref[...] = col(att, att)
        v_ref[...] = col(2 * att, att)
    a = col(3 * att, conv)
    g = col(3 * att + conv, conv)
    u_ref[...] = a * _sigmoid(g)


def _inproj(x, g, w, att, conv):
    t, d = x.shape
    tm = _tile(t, 512)
    row = lambda n: pl.BlockSpec((tm, n), lambda i: (i, 0))
    full = lambda a: pl.BlockSpec(a.shape, lambda i: (0, 0))
    return pl.pallas_call(
        functools.partial(_inproj_kernel, att=att, conv=conv, kv_t=False),
        grid=(t // tm,),
        in_specs=[row(d), full(g), full(w)],
        out_specs=[row(att), row(att), row(att), row(conv)],
        out_shape=[jax.ShapeDtypeStruct((t, att), F32)] * 3 + [jax.ShapeDtypeStruct((t, conv), F32)],
        compiler_params=_params("parallel"),
        name="inproj",
    )(x, g, w)


def _inproj_kv_t(x, g, w, wkv_t, att, conv, layer, depth, batch, kv_prev):
    t, d = x.shape
    seq = t // batch
    tm = _tile(seq, 512)
    per_seq = seq // tm
    row = lambda n: pl.BlockSpec((tm, n), lambda i: (i, 0))
    full = lambda a: pl.BlockSpec(a.shape, lambda i: (0, 0))
    slab = pl.BlockSpec((None, None, att, tm), lambda i: (layer, i // per_seq, 0, i % per_seq))
    kv_shape = jax.ShapeDtypeStruct((depth, batch, att, seq), F32)
    prev = list(kv_prev) if kv_prev is not None else []
    return pl.pallas_call(
        functools.partial(_inproj_kernel, att=att, conv=conv, kv_t=True),
        grid=(t // tm,),
        in_specs=[row(d), full(g), full(w), full(wkv_t)] + [pl.BlockSpec(memory_space=pl.ANY)] * len(prev),
        out_specs=[row(att), slab, slab, row(conv)],
        out_shape=[jax.ShapeDtypeStruct((t, att), F32), kv_shape, kv_shape, jax.ShapeDtypeStruct((t, conv), F32)],
        input_output_aliases={4: 1, 5: 2} if prev else {},
        compiler_params=_params("parallel"),
        name="inproj_kv_t",
    )(x, g, w, wkv_t, *prev)


def _log_sigmoids(z):
    ls = jnp.minimum(z, 0.0) - jnp.log(1.0 + jnp.exp(-jnp.abs(z)))
    return ls, ls - z


def _later_than(n):
    j = lax.broadcasted_iota(jnp.int32, (n, n), 0)
    s = lax.broadcasted_iota(jnp.int32, (n, n), 1)
    return jnp.where(j > s, 1.0, 0.0).astype(MX)


def _sb_weights(z, tri, carry, mask):
    ls, lk = _log_sigmoids(z)
    if mask is not None:
        lk = jnp.where(mask, lk, 0.0)
    later = _dot(lk.astype(MX), tri)
    a = jnp.exp(ls + later + carry)
    if mask is not None:
        a = jnp.where(mask, a, 0.0)
    return a, carry + later[:, 0:1] + lk[:, 0:1]


def _sb_prompt_kernel(bias_ref, q_ref, k_ref, v_ref, o_ref, z_ref, a_ref, carry_ref, *, tq):
    hp = pl.program_id(1)
    qi = pl.program_id(2)
    first = lax.broadcasted_iota(jnp.int32, (tq, LANES), 1) < HEAD_DIM
    q = q_ref[...] * (HEAD_DIM ** -0.5)
    q2 = jnp.concatenate([jnp.where(first, q, 0.0), jnp.where(first, 0.0, q)], axis=0).astype(MX)
    bias_a = bias_ref[2 * hp]
    bias_b = bias_ref[2 * hp + 1]
    tri = _later_than(tq)
    r = lax.broadcasted_iota(jnp.int32, (2 * tq, tq), 0)
    c = lax.broadcasted_iota(jnp.int32, (2 * tq, tq), 1)
    causal = c < jnp.where(r < tq, r, r - tq)
    v_first = lax.broadcasted_iota(jnp.int32, (LANES, tq), 0) < HEAD_DIM

    def cols(j):
        return pl.ds(pl.multiple_of(j * tq, tq), tq)

    def logits(j):
        z = _dot(q2, k_ref[:, cols(j)].astype(MX))
        return jnp.concatenate([z[:tq] + bias_a, z[tq:] + bias_b], axis=0)

    def weighted_values(a2, j):
        vb = v_ref[:, cols(j)]
        v2 = jnp.concatenate([jnp.where(v_first, vb, 0.0), jnp.where(v_first, 0.0, vb)], axis=1).astype(MX)
        return _dot_nt(a2, v2)

    def keep(a, carry):
        a_ref[...] = jnp.concatenate([a[:tq], a[tq:]], axis=1).astype(MX)
        carry_ref[...] = carry

    keep(*_sb_weights(logits(qi), tri, jnp.zeros((2 * tq, 1), F32), causal))
    z_ref[...] = logits(jnp.maximum(qi - 1, 0))
    o_ref[...] = jnp.zeros((tq, LANES), F32)

    def body(n, c):
        j = qi - 1 - n
        o_ref[...] += weighted_values(a_ref[...], j + 1)
        z = z_ref[...]
        z_ref[...] = logits(jnp.maximum(j - 1, 0))
        keep(*_sb_weights(z, tri, carry_ref[...], None))
        return c

    lax.fori_loop(0, qi, body, 0)
    o_ref[...] += weighted_values(a_ref[...], 0)


def _sb_prompt(q, k_t, v_t, layer, bias):
    b, s, w = q.shape
    tq = _tile(s, 256)
    qspec = pl.BlockSpec((None, tq, LANES), lambda bi, hp, qi: (bi, qi, hp))
    kvspec = pl.BlockSpec((None, None, LANES, s), lambda bi, hp, qi: (layer, bi, hp, 0))
    return pl.pallas_call(
        functools.partial(_sb_prompt_kernel, tq=tq),
        grid=(b, w // LANES, s // tq),
        in_specs=[pl.BlockSpec(memory_space=pltpu.SMEM), qspec, kvspec, kvspec],
        out_specs=qspec,
        out_shape=jax.ShapeDtypeStruct((b, s, w), F32),
        scratch_shapes=[pltpu.VMEM((2 * tq, tq), F32), pltpu.VMEM((tq, 2 * tq), MX),
                        pltpu.VMEM((2 * tq, 1), F32)],
        compiler_params=_params("parallel", "parallel", "arbitrary"),
        name="sb_prompt",
    )(bias, q, k_t, v_t)


def _sb_paged_kernel(pt_ref, bias_ref, q_ref, kn_ref, vn_ref, *rest, n_heads, n_q, page):
    kpages = rest[:PAGES_PER_STEP]
    vpages = rest[PAGES_PER_STEP:2 * PAGES_PER_STEP]
    o_ref, qbd_ref, acc_ref, new_ref, carry_ref, w_ref = rest[2 * PAGES_PER_STEP:]
    step = pl.program_id(1)
    rows = n_q * n_heads
    width = n_heads * HEAD_DIM
    pairs = PAGES_PER_STEP // 2
    bias = bias_ref[...]
    tri = _later_than(2 * page)

    @pl.when(step == 0)
    def _init():
        q = q_ref[...] * (HEAD_DIM ** -0.5)
        rep = jnp.concatenate([jnp.broadcast_to(q[t:t + 1, :], (n_heads, width)) for t in range(n_q)], axis=0)
        row = lax.broadcasted_iota(jnp.int32, (rows, width), 0)
        lane = lax.broadcasted_iota(jnp.int32, (rows, width), 1)
        qbd = jnp.where(lane // HEAD_DIM == row % n_heads, rep, 0.0)
        qbd_ref[...] = qbd
        pad = jnp.zeros((page - n_q, width), F32)
        kn = jnp.concatenate([kn_ref[...], pad], axis=0)
        vn = jnp.concatenate([vn_ref[...], pad], axis=0)
        key = lax.broadcasted_iota(jnp.int32, (rows, page), 1)
        t = lax.broadcasted_iota(jnp.int32, (rows, page), 0) // n_heads
        z = _dot_nt(qbd.astype(MX), kn.astype(MX)) + bias
        a, carry = _sb_weights(z, tri[:page, :page], jnp.zeros((rows, 1), F32), key < t)
        new_ref[...] = _dot(a.astype(MX), vn.astype(MX))
        carry_ref[...] = carry
        acc_ref[...] = jnp.zeros(acc_ref.shape, F32)
        w_ref[...] = jnp.zeros(w_ref.shape, MX)

    w_prev = w_ref[...]
    qbd = qbd_ref[...].astype(MX)
    scores = []
    for i in range(pairs):
        keys = jnp.concatenate([kpages[2 * i][...], kpages[2 * i + 1][...]], axis=1)
        scores.append(_log_sigmoids(_dot(qbd, keys.astype(MX)) + bias))
    later = _dot(jnp.concatenate([lk for _, lk in scores], axis=0).astype(MX), tri)
    carry = carry_ref[...]
    weights = [None] * pairs
    for i in reversed(range(pairs)):
        ls, lk = scores[i]
        lt = later[i * rows:(i + 1) * rows]
        weights[i] = jnp.exp(ls + lt + carry)
        carry = carry + lt[:, 0:1] + lk[:, 0:1]
    carry_ref[...] = carry
    w_ref[...] = jnp.concatenate(weights, axis=1).astype(MX)
    values = jnp.concatenate([vpages[p][...] for p in range(PAGES_PER_STEP)], axis=1)
    acc_ref[...] += _dot_nt(values.astype(MX), w_prev)

    @pl.when(step == pl.num_programs(1) - 1)
    def _finish():
        r = lax.broadcasted_iota(jnp.int32, (width, rows), 0)
        c = lax.broadcasted_iota(jnp.int32, (width, rows), 1)
        hi, lo = _split_bf16(jnp.where(c % n_heads == r // HEAD_DIM, acc_ref[...], 0.0))
        t = lax.broadcasted_iota(jnp.int32, (8, rows), 0)
        cc = lax.broadcasted_iota(jnp.int32, (8, rows), 1)
        pick = jnp.where(cc // n_heads == t, 1.0, 0.0).astype(BF16)
        past = _dot_nt(pick, hi) + _dot_nt(pick, lo)
        row = lax.broadcasted_iota(jnp.int32, (rows, width), 0)
        lane = lax.broadcasted_iota(jnp.int32, (rows, width), 1)
        own = jnp.where(lane // HEAD_DIM == row % n_heads, new_ref[...], 0.0)
        o_ref[...] = past[:n_q] + jnp.sum(own.reshape(n_q, n_heads, width), axis=1)


def _sb_paged(q, k_new, v_new, cache_kt, cache_vt, layer, page_table, bias):
    db, n_q, width = q.shape
    n_heads = width // HEAD_DIM
    page = cache_kt.shape[3]
    n_pages = page_table.shape[1]
    steps = n_pages // PAGES_PER_STEP
    rows = n_q * n_heads
    bias_col = jnp.tile(bias, n_q).reshape(rows, 1)

    def page_spec(p, delay):
        def index(bi, si, pt):
            group = steps - 1 - jnp.clip(si - delay, 0, steps - 1)
            return (layer, pt[bi, group * PAGES_PER_STEP + p], 0, 0)
        return pl.BlockSpec((None, None, width, page), index)

    small = pl.BlockSpec((None, n_q, width), lambda bi, si, pt: (bi, 0, 0))
    kpages = [page_spec(p, 0) for p in range(PAGES_PER_STEP)]
    vpages = [page_spec(p, 1) for p in range(PAGES_PER_STEP)]
    grid_spec = pltpu.PrefetchScalarGridSpec(
        num_scalar_prefetch=1,
        grid=(db, steps + 1),
        in_specs=[pl.BlockSpec((rows, 1), lambda bi, si, pt: (0, 0)), small, small, small] + kpages + vpages,
        out_specs=small,
        scratch_shapes=[pltpu.VMEM((rows, width), F32), pltpu.VMEM((width, rows), F32),
                        pltpu.VMEM((rows, width), F32), pltpu.VMEM((rows, 1), F32),
                        pltpu.VMEM((rows, PAGES_PER_STEP * page), MX)],
    )
    return pl.pallas_call(
        functools.partial(_sb_paged_kernel, n_heads=n_heads, n_q=n_q, page=page),
        grid_spec=grid_spec,
        out_shape=jax.ShapeDtypeStruct((db, n_q, width), F32),
        compiler_params=_params("parallel", "arbitrary"),
        name="sb_paged",
    )(page_table, bias_col, q, k_new, v_new, *([cache_kt] * PAGES_PER_STEP), *([cache_vt] * PAGES_PER_STEP))


CONV_PAD = 32


def _conv_kernel(ctx_ref, u_ref, w_ref, b_ref, g_ref, nb_ref, o_ref, ext_ref, *, taps, seq, rows):
    n_ctx = taps - 1
    off = CONV_PAD - n_ctx
    ext_ref[0:CONV_PAD, :] = jnp.zeros((CONV_PAD, ext_ref.shape[1]), F32)
    ext_ref[off:CONV_PAD, :] = ctx_ref[...]
    if seq % 8:
        ext_ref[CONV_PAD:, :] = jnp.zeros((ext_ref.shape[0] - CONV_PAD, ext_ref.shape[1]), F32)
    ext_ref[CONV_PAD:CONV_PAD + seq, :] = u_ref[...]
    w = w_ref[...]
    out_rows = min(rows, seq)

    def chunk(i, _):
        base = pl.multiple_of(i * rows, rows)
        span = rows + CONV_PAD
        parts = []
        for c0 in range(0, ext_ref.shape[1], LANES):
            win = ext_ref[pl.ds(base, span), c0:c0 + LANES]
            part = jnp.zeros((rows, LANES), F32) + b_ref[:, c0:c0 + LANES]
            for s in range(8):
                shifted = win if s == 0 else pltpu.roll(win, span - s, axis=0)
                for k in range(taps):
                    o = off + k - s
                    if o % 8 == 0:
                        part = part + w[k:k + 1, c0:c0 + LANES] * shifted[o:o + rows]
            parts.append(part)
        acc = jnp.concatenate(parts, axis=1)
        mu = jnp.mean(acc, axis=-1, keepdims=True)
        xc = acc - mu
        y = xc * lax.rsqrt(jnp.mean(xc * xc, axis=-1, keepdims=True) + EPS) * g_ref[...] + nb_ref[...]
        y = y * _sigmoid(y)
        o_ref[pl.ds(base, out_rows), :] = y[:out_rows]
        return 0

    lax.fori_loop(0, max(seq // rows, 1), chunk, 0)


def _conv_module(ctx, u, w, b, g, nb):
    bsz, seq, ch = u.shape
    taps = w.shape[0]
    rows = 128 if seq >= 128 else 8
    ext_rows = CONV_PAD + max(seq, rows)
    vec = lambda a: pl.BlockSpec(a.shape, lambda i: (0, 0))
    return pl.pallas_call(
        functools.partial(_conv_kernel, taps=taps, seq=seq, rows=rows),
        grid=(bsz,),
        in_specs=[pl.BlockSpec((None, taps - 1, ch), lambda i: (i, 0, 0)),
                  pl.BlockSpec((None, seq, ch), lambda i: (i, 0, 0)),
                  vec(w), vec(b), vec(g), vec(nb)],
        out_specs=pl.BlockSpec((None, seq, ch), lambda i: (i, 0, 0)),
        out_shape=jax.ShapeDtypeStruct((bsz, seq, ch), F32),
        scratch_shapes=[pltpu.VMEM((ext_rows, ch), F32)],
        compiler_params=_params("parallel"),
        name="conv_module",
    )(ctx, u, w, b, g, nb)


def _outproj_kernel(x_ref, att_ref, conv_ref, wa_ref, wc_ref, o_ref):
    o_ref[...] = (x_ref[...] + _dot(att_ref[...].astype(BF16), wa_ref[...])
                  + _dot(conv_ref[...].astype(BF16), wc_ref[...]))


def _outproj(x, att, conv, wa, wc):
    t, d = x.shape
    tm = _tile(t, 512)
    row = lambda a: pl.BlockSpec((tm, a.shape[1]), lambda i: (i, 0))
    full = lambda a: pl.BlockSpec(a.shape, lambda i: (0, 0))
    return pl.pallas_call(
        _outproj_kernel,
        grid=(t // tm,),
        in_specs=[row(x), row(att), row(conv), full(wa), full(wc)],
        out_specs=row(x),
        out_shape=jax.ShapeDtypeStruct((t, d), F32),
        compiler_params=_params("parallel"),
        name="outproj",
    )(x, att, conv, wa, wc)


def _swiglu_hidden(hn, wg_ref, wu_ref, h_ref):
    d_ff = h_ref.shape[1]
    for c0 in range(0, d_ff, FF_CHUNK):
        g = _dot(hn, wg_ref[:, c0:c0 + FF_CHUNK].astype(BF16))
        u = _dot(hn, wu_ref[:, c0:c0 + FF_CHUNK].astype(BF16))
        h_ref[:, c0:c0 + FF_CHUNK] = (g * _sigmoid(g) * u).astype(BF16)


def _ffn_kernel(x_ref, g_ref, wg_ref, wu_ref, wd_ref, o_ref, h_ref):
    x = x_ref[...]
    _swiglu_hidden(_rms(x, g_ref[...]).astype(BF16), wg_ref, wu_ref, h_ref)
    o_ref[...] = x + _dot(h_ref[...], wd_ref[...])


def _resident(shape, index):
    return pl.BlockSpec(shape, index, pipeline_mode=pl.Buffered(1))


def _ffn(x, g, wg, wu, wd):
    t, d = x.shape
    d_ff = wg.shape[1]
    tm = _tile(t, 512)
    row = pl.BlockSpec((tm, d), lambda i: (i, 0))
    return pl.pallas_call(
        _ffn_kernel,
        grid=(t // tm,),
        in_specs=[row, pl.BlockSpec(g.shape, lambda i: (0, 0)),
                  _resident(wg.shape, lambda i: (0, 0)), _resident(wu.shape, lambda i: (0, 0)),
                  _resident(wd.shape, lambda i: (0, 0))],
        out_specs=row,
        out_shape=jax.ShapeDtypeStruct((t, d), F32),
        scratch_shapes=[pltpu.VMEM((tm, d_ff), BF16)],
        compiler_params=_params("parallel"),
        name="ffn_dense",
    )(x, g, wg, wu, wd)


def _expert_kernel(be_ref, x_ref, wg_ref, wu_ref, wd_ref, o_ref, h_ref):
    _swiglu_hidden(x_ref[...], wg_ref, wu_ref, h_ref)
    o_ref[...] = _dot(h_ref[...], wd_ref[...].astype(BF16))


def _expert_blocks(xg, block_e, wg, wu, wd):
    rows, d = xg.shape
    d_ff = wg.shape[2]
    grid_spec = pltpu.PrefetchScalarGridSpec(
        num_scalar_prefetch=1,
        grid=(rows // MOE_ROWS,),
        in_specs=[pl.BlockSpec((MOE_ROWS, d), lambda i, be: (i, 0)),
                  _resident((None, d, d_ff), lambda i, be: (be[i], 0, 0)),
                  _resident((None, d, d_ff), lambda i, be: (be[i], 0, 0)),
                  _resident((None, d_ff, d), lambda i, be: (be[i], 0, 0))],
        out_specs=pl.BlockSpec((MOE_ROWS, d), lambda i, be: (i, 0)),
        scratch_shapes=[pltpu.VMEM((MOE_ROWS, d_ff), BF16)],
    )
    return pl.pallas_call(
        _expert_kernel,
        grid_spec=grid_spec,
        out_shape=jax.ShapeDtypeStruct((rows, d), F32),
        compiler_params=_params("arbitrary"),
        name="moe_experts",
    )(block_e, xg, wg, wu, wd)


def _router_kernel(x_ref, g_ref, r_ref, hn_ref, route_ref, *, n_experts):
    hn = _rms(x_ref[...], g_ref[...])
    hn_ref[...] = hn.astype(BF16)
    h_hi, h_lo = _split_bf16(hn)
    r_hi, r_lo = _split_bf16(r_ref[...])
    logits = _dot(h_hi, r_hi) + (_dot(h_hi, r_lo) + _dot(h_lo, r_hi))
    lane = lax.broadcasted_iota(jnp.int32, logits.shape, 1).astype(F32)
    neg = jnp.float32(-jnp.inf)
    logits = jnp.where(lane < n_experts, logits, neg)
    v1 = jnp.max(logits, axis=-1, keepdims=True)
    i1 = jnp.min(jnp.where(logits == v1, lane, float(LANES)), axis=-1, keepdims=True)
    rest = jnp.where(lane == i1, neg, logits)
    v2 = jnp.max(rest, axis=-1, keepdims=True)
    i2 = jnp.min(jnp.where(rest == v2, lane, float(LANES)), axis=-1, keepdims=True)
    e = jnp.exp(v2 - v1)
    g1 = 1.0 / (1.0 + e)
    g2 = e / (1.0 + e)
    route_ref[...] = jnp.where(lane == 0, i1, jnp.where(lane == 1, i2, jnp.where(lane == 2, g1,
                               jnp.where(lane == 3, g2, 0.0))))


def _router(x, g, router_pad, n_experts):
    t, d = x.shape
    tm = _tile(t, 512)
    row = lambda n: pl.BlockSpec((tm, n), lambda i: (i, 0))
    return pl.pallas_call(
        functools.partial(_router_kernel, n_experts=n_experts),
        grid=(t // tm,),
        in_specs=[row(d), pl.BlockSpec(g.shape, lambda i: (0, 0)),
                  pl.BlockSpec(router_pad.shape, lambda i: (0, 0))],
        out_specs=[row(d), row(LANES)],
        out_shape=[jax.ShapeDtypeStruct((t, d), BF16), jax.ShapeDtypeStruct((t, LANES), F32)],
        compiler_params=_params("parallel"),
        name="router",
    )(x, g, router_pad)


def _moe_dispatch(x, g, router):
    t, d = x.shape
    n_experts = router.shape[1]
    router_pad = jnp.pad(router, ((0, 0), (0, LANES - n_experts)))
    hn, route = _router(x, g, router_pad, n_experts)
    flat_e = route[:, :TOP_K].astype(jnp.int32).reshape(-1)
    n_assign = t * TOP_K
    onehot = (flat_e[:, None] == jnp.arange(n_experts, dtype=jnp.int32)[None, :]).astype(jnp.int32)
    before = jnp.cumsum(onehot, axis=0) - onehot
    counts = jnp.sum(onehot, axis=0)
    padded = (counts + MOE_ROWS - 1) // MOE_ROWS * MOE_ROWS
    pad_end = jnp.cumsum(padded)
    pad_start = pad_end - padded
    dest = jnp.sum(onehot * (before + pad_start[None, :]), axis=1)
    n_blocks = -(-n_assign // MOE_ROWS) + n_experts
    rows = n_blocks * MOE_ROWS
    row_tok = jnp.zeros((rows,), jnp.int32).at[dest].set(
        jnp.arange(n_assign, dtype=jnp.int32) // TOP_K, unique_indices=True, mode='promise_in_bounds')
    block_start = jnp.arange(n_blocks, dtype=jnp.int32) * MOE_ROWS
    block_e = jnp.minimum(jnp.sum((pad_end[None, :] <= block_start[:, None]).astype(jnp.int32), axis=1),
                          n_experts - 1)
    return hn[row_tok], block_e, dest.reshape(t, TOP_K), route


def _moe_experts(xg, block_e, pos, route, wg, wu, wd):
    y = _expert_blocks(xg, block_e, wg, wu, wd)
    return [y[pos[:, 0]], y[pos[:, 1]], route]


def _ple_kernel(*refs, n_add, final):
    x_ref = refs[0]
    p_ref, g_ref, wgate_ref, wproj_ref = refs[1 + n_add:5 + n_add]
    fg_ref = refs[5 + n_add] if final else None
    o_ref = refs[-1]
    x = x_ref[...]
    if n_add:
        ya_ref, yb_ref, route_ref = refs[1:4]
        route = route_ref[...]
        x = x + (ya_ref[...] * route[:, TOP_K:TOP_K + 1] + yb_ref[...] * route[:, TOP_K + 1:TOP_K + 2])
    gate = _sigmoid(_dot(_rms(x, g_ref[...]).astype(BF16), wgate_ref[...]))
    x = x + gate * _dot(p_ref[...].astype(BF16), wproj_ref[...])
    o_ref[...] = _rms(x, fg_ref[...]) if final else x


def _ple(x, adds, p, g, wgate, wproj, final_g):
    t, d = x.shape
    tm = _tile(t, 512)
    row = lambda a: pl.BlockSpec((tm, a.shape[1]), lambda i: (i, 0))
    full = lambda a: pl.BlockSpec(a.shape, lambda i: (0, 0))
    final = final_g is not None
    ops = [x, *adds, p, g, wgate, wproj] + ([final_g] if final else [])
    specs = [row(x)] + [row(a) for a in adds] + [row(p), full(g), full(wgate), full(wproj)]
    specs += [full(final_g)] if final else []
    return pl.pallas_call(
        functools.partial(_ple_kernel, n_add=len(adds), final=final),
        grid=(t // tm,),
        in_specs=specs,
        out_specs=row(x),
        out_shape=jax.ShapeDtypeStruct((t, d), F32),
        compiler_params=_params("parallel"),
        name="ple",
    )(*ops)


def kernel(x_prompt, x_sample, cache_k, cache_v, state_conv, page_table, p_prompt, p_sample,
           w_in, sb_bias, w_out, conv_w, conv_b, conv_norm_g, conv_norm_b, norm_mix_g, norm_ffn_g,
           norm_ple_g, w_ple_gate, w_ple_proj, ffn_w_gate, ffn_w_up, ffn_w_down,
           moe_router, moe_w_gate, moe_w_up, moe_w_down, final_norm_g):
    depth, d_model = norm_mix_g.shape
    n_heads = sb_bias.shape[1]
    att = n_heads * HEAD_DIM
    conv = conv_w.shape[2]
    taps = conv_w.shape[1]
    n_pool, page = cache_k.shape[1], cache_k.shape[2]
    cache_kt = jnp.transpose(cache_k, (0, 1, 3, 4, 2)).reshape(depth, n_pool, att, page)
    cache_vt = jnp.transpose(cache_v, (0, 1, 3, 4, 2)).reshape(depth, n_pool, att, page)
    vec = lambda a: a.reshape(1, -1)
    final_g = vec(final_norm_g)
    bsz, seq, _ = x_prompt.shape

    def mix_in(i, x, a, u3, ctx):
        b, s, _ = u3.shape
        c3 = _conv_module(ctx, u3, conv_w[i], vec(conv_b[i]), vec(conv_norm_g[i]), vec(conv_norm_b[i]))
        wo = w_out[i].astype(BF16)
        x = _outproj(x, a, c3.reshape(b * s, conv), wo[:att], wo[att:])
        routed = _moe_dispatch(x, vec(norm_ffn_g[i]), moe_router[i // 2]) if i % 2 else None
        return x, routed, jnp.concatenate([ctx, u3], axis=1)[:, -(taps - 1):]

    def mix_out(i, x, routed, p3):
        j = i // 2
        if i % 2 == 0:
            x = _ffn(x, vec(norm_ffn_g[i]), ffn_w_gate[j].astype(BF16), ffn_w_up[j].astype(BF16),
                     ffn_w_down[j].astype(BF16))
            adds = []
        else:
            adds = _moe_experts(*routed, moe_w_gate[j], moe_w_up[j], moe_w_down[j])
        return _ple(x, adds, p3.reshape(x.shape[0], -1), vec(norm_ple_g[i]), w_ple_gate[i].astype(BF16),
                    w_ple_proj[i].astype(BF16), final_g if i == depth - 1 else None)

    xp = x_prompt.reshape(bsz * seq, d_model)
    db, n_q, _ = x_sample.shape
    xs = x_sample.reshape(db * n_q, d_model)
    conv_zero = jnp.zeros((bsz, taps - 1, conv), F32)
    kv_t = None
    cp_l, ks_l, vs_l, cs_l = [], [], [], []
    for i in range(depth):
        g = vec(norm_mix_g[i])
        w = w_in[i].astype(BF16)
        wkv_t = w_in[i][:, att:3 * att].T.astype(BF16)
        q, k_t, v_t, u = _inproj_kv_t(xp, g, w, wkv_t, att, conv, i, depth, bsz, kv_t)
        kv_t = (k_t, v_t)
        a = _sb_prompt(q.reshape(bsz, seq, att), k_t, v_t, i, sb_bias[i]).reshape(bsz * seq, att)
        xp, routed_p, cp = mix_in(i, xp, a, u.reshape(bsz, seq, conv), conv_zero)

        q, k, v, u = _inproj(xs, g, w, att, conv)
        q3, k3, v3 = (t.reshape(db, n_q, att) for t in (q, k, v))
        a = _sb_paged(q3, k3, v3, cache_kt, cache_vt, i, page_table, sb_bias[i]).reshape(db * n_q, att)
        xs, routed_s, cs = mix_in(i, xs, a, u.reshape(db, n_q, conv), state_conv[i])

        xp = mix_out(i, xp, routed_p, p_prompt[i])
        xs = mix_out(i, xs, routed_s, p_sample[i])
        cp_l.append(cp)
        ks_l.append(k3.reshape(db, n_q, n_heads, HEAD_DIM))
        vs_l.append(v3.reshape(db, n_q, n_heads, HEAD_DIM))
        cs_l.append(cs)

    heads_last = lambda t: jnp.transpose(t.reshape(depth, bsz, n_heads, HEAD_DIM, seq), (0, 1, 4, 2, 3))
    return (xp.reshape(bsz, seq, d_model), xs.reshape(db, n_q, d_model), heads_last(kv_t[0]), heads_last(kv_t[1]),
            jnp.stack(cp_l), jnp.stack(ks_l), jnp.stack(vs_l), jnp.stack(cs_l))
```

```python
import functools

import jax
import jax.numpy as jnp
from jax import lax
from jax.experimental import pallas as pl
from jax.experimental.pallas import tpu as pltpu

F32 = jnp.float32
BF16 = jnp.bfloat16
MX = jnp.bfloat16
EPS = 1e-6
HEAD_DIM = 64
LANES = 128
TOP_K = 2
VMEM_LIMIT = 56 * 1024 * 1024
PAGES_PER_STEP = 16
MOE_ROWS = 256
FF_CHUNK = 256


def _params(*sem):
    return pltpu.CompilerParams(dimension_semantics=sem, vmem_limit_bytes=VMEM_LIMIT)


def _tile(n, pref):
    if n <= pref:
        return n
    t = pref - pref % 8
    while n % t:
        t -= 8
    return t


def _dot(a, b):
    return jnp.dot(a, b, preferred_element_type=F32)


def _dot_nt(a, b):
    return lax.dot_general(a, b, (((1,), (1,)), ((), ())), preferred_element_type=F32)


def _rms(xf, g):
    return xf * lax.rsqrt(jnp.mean(xf * xf, axis=-1, keepdims=True) + EPS) * g


def _sigmoid(x):
    return 1.0 / (1.0 + jnp.exp(-x))


def _split_bf16(x):
    hi = x.astype(BF16)
    lo = (x - hi.astype(F32)).astype(BF16)
    return hi, lo


def _inproj_kernel(*refs, att, conv, kv_t, n_prev=0):
    x_ref, g_ref, w_ref = refs[:3]
    q_ref, k_ref, v_ref, u_ref = refs[-4:]
    hn = _rms(x_ref[...], g_ref[...]).astype(BF16)
    col = lambda c0, n: _dot(hn, w_ref[:, c0:c0 + n])
    q_ref[...] = col(0, att)
    if kv_t:
        kv = _dot_nt(refs[3][...], hn)
        k_ref[n_prev] = kv[:att]
        v_ref[n_prev] = kv[att:]
        if n_prev:
            k_ref[0:n_prev] = refs[4][...]
            v_ref[0:n_prev] = refs[5][...]
    else:
        k_ref[...] = col(att, att)
        v_ref[...] = col(2 * att, att)
    a = col(3 * att, conv)
    g = col(3 * att + conv, conv)
    u_ref[...] = a * _sigmoid(g)


def _inproj(x, g, w, att, conv):
    t, d = x.shape
    tm = _tile(t, 512)
    row = lambda n: pl.BlockSpec((tm, n), lambda i: (i, 0))
    full = lambda a: pl.BlockSpec(a.shape, lambda i: (0, 0))
    return pl.pallas_call(
        functools.partial(_inproj_kernel, att=att, conv=conv, kv_t=False),
        grid=(t // tm,),
        in_specs=[row(d), full(g), full(w)],
        out_specs=[row(att), row(att), row(att), row(conv)],
        out_shape=[jax.ShapeDtypeStruct((t, att), F32)] * 3 + [jax.ShapeDtypeStruct((t, conv), F32)],
        compiler_params=_params("parallel"),
        name="inproj",
    )(x, g, w)


def _inproj_kv_t(x, g, w, wkv_t, att, conv, batch, kv_prev):
    t, d = x.shape
    seq = t // batch
    tm = _tile(seq, 512)
    per_seq = seq // tm
    prev = list(kv_prev) if kv_prev is not None else []
    n_prev = prev[0].shape[0] if prev else 0
    row = lambda n: pl.BlockSpec((tm, n), lambda i: (i, 0))
    full = lambda a: pl.BlockSpec(a.shape, lambda i: (0, 0))
    slabs = lambda n: pl.BlockSpec((n, None, att, tm), lambda i: (0, i // per_seq, 0, i % per_seq))
    kv_shape = jax.ShapeDtypeStruct((n_prev + 1, batch, att, seq), F32)
    return pl.pallas_call(
        functools.partial(_inproj_kernel, att=att, conv=conv, kv_t=True, n_prev=n_prev),
        grid=(t // tm,),
        in_specs=[row(d), full(g), full(w), full(wkv_t)] + [slabs(n_prev)] * len(prev),
        out_specs=[row(att), slabs(n_prev + 1), slabs(n_prev + 1), row(conv)],
        out_shape=[jax.ShapeDtypeStruct((t, att), F32), kv_shape, kv_shape, jax.ShapeDtypeStruct((t, conv), F32)],
        compiler_params=_params("parallel"),
        name="inproj_kv_t",
    )(x, g, w, wkv_t, *prev)


def _log_sigmoids(z):
    ls = jnp.minimum(z, 0.0) - jnp.log(1.0 + jnp.exp(-jnp.abs(z)))
    return ls, ls - z


def _later_than(n):
    j = lax.broadcasted_iota(jnp.int32, (n, n), 0)
    s = lax.broadcasted_iota(jnp.int32, (n, n), 1)
    return jnp.where(j > s, 1.0, 0.0).astype(MX)


def _sb_weights(z, tri, carry, mask):
    ls, lk = _log_sigmoids(z)
    if mask is not None:
        lk = jnp.where(mask, lk, 0.0)
    later = _dot(lk.astype(MX), tri)
    a = jnp.exp(ls + later + carry)
    if mask is not None:
        a = jnp.where(mask, a, 0.0)
    return a, carry + later[:, 0:1] + lk[:, 0:1]


def _sb_prompt_kernel(bias_ref, q_ref, k_ref, v_ref, o_ref, z_ref, a_ref, carry_ref, *, tq):
    hp = pl.program_id(1)
    qi = pl.program_id(2)
    first = lax.broadcasted_iota(jnp.int32, (tq, LANES), 1) < HEAD_DIM
    q = q_ref[...] * (HEAD_DIM ** -0.5)
    q2 = jnp.concatenate([jnp.where(first, q, 0.0), jnp.where(first, 0.0, q)], axis=0).astype(MX)
    bias_a = bias_ref[2 * hp]
    bias_b = bias_ref[2 * hp + 1]
    tri = _later_than(tq)
    r = lax.broadcasted_iota(jnp.int32, (2 * tq, tq), 0)
    c = lax.broadcasted_iota(jnp.int32, (2 * tq, tq), 1)
    causal = c < jnp.where(r < tq, r, r - tq)
    v_first = lax.broadcasted_iota(jnp.int32, (LANES, tq), 0) < HEAD_DIM

    def cols(j):
        return pl.ds(pl.multiple_of(j * tq, tq), tq)

    def logits(j):
        z = _dot(q2, k_ref[:, cols(j)].astype(MX))
        return jnp.concatenate([z[:tq] + bias_a, z[tq:] + bias_b], axis=0)

    def weighted_values(a2, j):
        vb = v_ref[:, cols(j)]
        v2 = jnp.concatenate([jnp.where(v_first, vb, 0.0), jnp.where(v_first, 0.0, vb)], axis=1).astype(MX)
        return _dot_nt(a2, v2)

    def keep(a, carry):
        a_ref[...] = jnp.concatenate([a[:tq], a[tq:]], axis=1).astype(MX)
        carry_ref[...] = carry

    keep(*_sb_weights(logits(qi), tri, jnp.zeros((2 * tq, 1), F32), causal))
    z_ref[...] = logits(jnp.maximum(qi - 1, 0))
    o_ref[...] = jnp.zeros((tq, LANES), F32)

    def body(n, c):
        j = qi - 1 - n
        o_ref[...] += weighted_values(a_ref[...], j + 1)
        z = z_ref[...]
        z_ref[...] = logits(jnp.maximum(j - 1, 0))
        keep(*_sb_weights(z, tri, carry_ref[...], None))
        return c

    lax.fori_loop(0, qi, body, 0)
    o_ref[...] += weighted_values(a_ref[...], 0)


def _sb_prompt(q, k_t, v_t, bias):
    b, s, w = q.shape
    tq = _tile(s, 256)
    layer = k_t.shape[0] - 1
    qspec = pl.BlockSpec((None, tq, LANES), lambda bi, hp, qi: (bi, qi, hp))
    kvspec = pl.BlockSpec((None, None, LANES, s), lambda bi, hp, qi: (layer, bi, hp, 0))
    return pl.pallas_call(
        functools.partial(_sb_prompt_kernel, tq=tq),
        grid=(b, w // LANES, s // tq),
        in_specs=[pl.BlockSpec(memory_space=pltpu.SMEM), qspec, kvspec, kvspec],
        out_specs=qspec,
        out_shape=jax.ShapeDtypeStruct((b, s, w), F32),
        scratch_shapes=[pltpu.VMEM((2 * tq, tq), F32), pltpu.VMEM((tq, 2 * tq), MX),
                        pltpu.VMEM((2 * tq, 1), F32)],
        compiler_params=_params("parallel", "parallel", "arbitrary"),
        name="sb_prompt",
    )(bias, q, k_t, v_t)


def _sb_paged_kernel(pt_ref, bias_ref, q_ref, kn_ref, vn_ref, *rest, n_heads, n_q, page):
    kpages = rest[:PAGES_PER_STEP]
    vpages = rest[PAGES_PER_STEP:2 * PAGES_PER_STEP]
    o_ref, qbd_ref, acc_ref, new_ref, carry_ref, w_ref = rest[2 * PAGES_PER_STEP:]
    step = pl.program_id(1)
    rows = n_q * n_heads
    width = n_heads * HEAD_DIM
    pairs = PAGES_PER_STEP // 2
    bias = bias_ref[...]
    tri = _later_than(2 * page)

    @pl.when(step == 0)
    def _init():
        q = q_ref[...] * (HEAD_DIM ** -0.5)
        rep = jnp.concatenate([jnp.broadcast_to(q[t:t + 1, :], (n_heads, width)) for t in range(n_q)], axis=0)
        row = lax.broadcasted_iota(jnp.int32, (rows, width), 0)
        lane = lax.broadcasted_iota(jnp.int32, (rows, width), 1)
        qbd = jnp.where(lane // HEAD_DIM == row % n_heads, rep, 0.0)
        qbd_ref[...] = qbd
        pad = jnp.zeros((page - n_q, width), F32)
        kn = jnp.concatenate([kn_ref[...], pad], axis=0)
        vn = jnp.concatenate([vn_ref[...], pad], axis=0)
        key = lax.broadcasted_iota(jnp.int32, (rows, page), 1)
        t = lax.broadcasted_iota(jnp.int32, (rows, page), 0) // n_heads
        z = _dot_nt(qbd.astype(MX), kn.astype(MX)) + bias
        a, carry = _sb_weights(z, tri[:page, :page], jnp.zeros((rows, 1), F32), key < t)
        new_ref[...] = _dot(a.astype(MX), vn.astype(MX))
        carry_ref[...] = carry
        acc_ref[...] = jnp.zeros(acc_ref.shape, F32)
        w_ref[...] = jnp.zeros(w_ref.shape, MX)

    w_prev = w_ref[...]
    qbd = qbd_ref[...].astype(MX)
    scores = []
    for i in range(pairs):
        keys = jnp.concatenate([kpages[2 * i][...], kpages[2 * i + 1][...]], axis=1)
        scores.append(_log_sigmoids(_dot(qbd, keys.astype(MX)) + bias))
    later = _dot(jnp.concatenate([lk for _, lk in scores], axis=0).astype(MX), tri)
    carry = carry_ref[...]
    weights = [None] * pairs
    for i in reversed(range(pairs)):
        ls, lk = scores[i]
        lt = later[i * rows:(i + 1) * rows]
        weights[i] = jnp.exp(ls + lt + carry)
        carry = carry + lt[:, 0:1] + lk[:, 0:1]
    carry_ref[...] = carry
    w_ref[...] = jnp.concatenate(weights, axis=1).astype(MX)
    values = jnp.concatenate([vpages[p][...] for p in range(PAGES_PER_STEP)], axis=1)
    acc_ref[...] += _dot_nt(values.astype(MX), w_prev)

    @pl.when(step == pl.num_programs(1) - 1)
    def _finish():
        r = lax.broadcasted_iota(jnp.int32, (width, rows), 0)
        c = lax.broadcasted_iota(jnp.int32, (width, rows), 1)
        hi, lo = _split_bf16(jnp.where(c % n_heads == r // HEAD_DIM, acc_ref[...], 0.0))
        t = lax.broadcasted_iota(jnp.int32, (8, rows), 0)
        cc = lax.broadcasted_iota(jnp.int32, (8, rows), 1)
        pick = jnp.where(cc // n_heads == t, 1.0, 0.0).astype(BF16)
        past = _dot_nt(pick, hi) + _dot_nt(pick, lo)
        row = lax.broadcasted_iota(jnp.int32, (rows, width), 0)
        lane = lax.broadcasted_iota(jnp.int32, (rows, width), 1)
        own = jnp.where(lane // HEAD_DIM == row % n_heads, new_ref[...], 0.0)
        o_ref[...] = past[:n_q] + jnp.sum(own.reshape(n_q, n_heads, width), axis=1)


def _sb_paged(q, k_new, v_new, cache_kt, cache_vt, layer, page_table, bias):
    db, n_q, width = q.shape
    n_heads = width // HEAD_DIM
    page = cache_kt.shape[3]
    n_pages = page_table.shape[1]
    steps = n_pages // PAGES_PER_STEP
    rows = n_q * n_heads
    bias_col = jnp.tile(bias, n_q).reshape(rows, 1)

    def page_spec(p, delay):
        def index(bi, si, pt):
            group = steps - 1 - jnp.clip(si - delay, 0, steps - 1)
            return (layer, pt[bi, group * PAGES_PER_STEP + p], 0, 0)
        return pl.BlockSpec((None, None, width, page), index)

    small = pl.BlockSpec((None, n_q, width), lambda bi, si, pt: (bi, 0, 0))
    kpages = [page_spec(p, 0) for p in range(PAGES_PER_STEP)]
    vpages = [page_spec(p, 1) for p in range(PAGES_PER_STEP)]
    grid_spec = pltpu.PrefetchScalarGridSpec(
        num_scalar_prefetch=1,
        grid=(db, steps + 1),
        in_specs=[pl.BlockSpec((rows, 1), lambda bi, si, pt: (0, 0)), small, small, small] + kpages + vpages,
        out_specs=small,
        scratch_shapes=[pltpu.VMEM((rows, width), F32), pltpu.VMEM((width, rows), F32),
                        pltpu.VMEM((rows, width), F32), pltpu.VMEM((rows, 1), F32),
                        pltpu.VMEM((rows, PAGES_PER_STEP * page), MX)],
    )
    return pl.pallas_call(
        functools.partial(_sb_paged_kernel, n_heads=n_heads, n_q=n_q, page=page),
        grid_spec=grid_spec,
        out_shape=jax.ShapeDtypeStruct((db, n_q, width), F32),
        compiler_params=_params("parallel", "arbitrary"),
        name="sb_paged",
    )(page_table, bias_col, q, k_new, v_new, *([cache_kt] * PAGES_PER_STEP), *([cache_vt] * PAGES_PER_STEP))


CONV_PAD = 32


def _conv_kernel(ctx_ref, u_ref, w_ref, b_ref, g_ref, nb_ref, o_ref, ext_ref, *, taps, seq, rows):
    n_ctx = taps - 1
    off = CONV_PAD - n_ctx
    ext_ref[0:CONV_PAD, :] = jnp.zeros((CONV_PAD, ext_ref.shape[1]), F32)
    ext_ref[off:CONV_PAD, :] = ctx_ref[...]
    if seq % 8:
        ext_ref[CONV_PAD:, :] = jnp.zeros((ext_ref.shape[0] - CONV_PAD, ext_ref.shape[1]), F32)
    ext_ref[CONV_PAD:CONV_PAD + seq, :] = u_ref[...]
    w = w_ref[...]
    out_rows = min(rows, seq)

    def chunk(i, _):
        base = pl.multiple_of(i * rows, rows)
        span = rows + CONV_PAD
        parts = []
        for c0 in range(0, ext_ref.shape[1], LANES):
            win = ext_ref[pl.ds(base, span), c0:c0 + LANES]
            part = jnp.zeros((rows, LANES), F32) + b_ref[:, c0:c0 + LANES]
            for s in range(8):
                shifted = win if s == 0 else pltpu.roll(win, span - s, axis=0)
                for k in range(taps):
                    o = off + k - s
                    if o % 8 == 0:
                        part = part + w[k:k + 1, c0:c0 + LANES] * shifted[o:o + rows]
            parts.append(part)
        acc = jnp.concatenate(parts, axis=1)
        mu = jnp.mean(acc, axis=-1, keepdims=True)
        xc = acc - mu
        y = xc * lax.rsqrt(jnp.mean(xc * xc, axis=-1, keepdims=True) + EPS) * g_ref[...] + nb_ref[...]
        y = y * _sigmoid(y)
        o_ref[pl.ds(base, out_rows), :] = y[:out_rows]
        return 0

    lax.fori_loop(0, max(seq // rows, 1), chunk, 0)


def _conv_module(ctx, u, w, b, g, nb):
    bsz, seq, ch = u.shape
    taps = w.shape[0]
    rows = 128 if seq >= 128 else 8
    ext_rows = CONV_PAD + max(seq, rows)
    vec = lambda a: pl.BlockSpec(a.shape, lambda i: (0, 0))
    return pl.pallas_call(
        functools.partial(_conv_kernel, taps=taps, seq=seq, rows=rows),
        grid=(bsz,),
        in_specs=[pl.BlockSpec((None, taps - 1, ch), lambda i: (i, 0, 0)),
                  pl.BlockSpec((None, seq, ch), lambda i: (i, 0, 0)),
                  vec(w), vec(b), vec(g), vec(nb)],
        out_specs=pl.BlockSpec((None, seq, ch), lambda i: (i, 0, 0)),
        out_shape=jax.ShapeDtypeStruct((bsz, seq, ch), F32),
        scratch_shapes=[pltpu.VMEM((ext_rows, ch), F32)],
        compiler_params=_params("parallel"),
        name="conv_module",
    )(ctx, u, w, b, g, nb)


def _mixed_residual(x_ref, att_ref, conv_ref, wa_ref, wc_ref):
    return (x_ref[...] + _dot(att_ref[...].astype(BF16), wa_ref[...])
            + _dot(conv_ref[...].astype(BF16), wc_ref[...]))


def _mixed_specs(tm, x, att, conv, wa, wc):
    row = lambda a: pl.BlockSpec((tm, a.shape[1]), lambda i: (i, 0))
    full = lambda a: pl.BlockSpec(a.shape, lambda i: (0, 0))
    return [row(x), row(att), row(conv), full(wa), full(wc)]


def _swiglu_hidden(hn, wg_ref, wu_ref, h_ref):
    d_ff = h_ref.shape[1]
    for c0 in range(0, d_ff, FF_CHUNK):
        g = _dot(hn, wg_ref[:, c0:c0 + FF_CHUNK].astype(BF16))
        u = _dot(hn, wu_ref[:, c0:c0 + FF_CHUNK].astype(BF16))
        h_ref[:, c0:c0 + FF_CHUNK] = (g * _sigmoid(g) * u).astype(BF16)


def _ffn_kernel(x_ref, att_ref, conv_ref, wa_ref, wc_ref, g_ref, wg_ref, wu_ref, wd_ref, o_ref, h_ref):
    x = _mixed_residual(x_ref, att_ref, conv_ref, wa_ref, wc_ref)
    _swiglu_hidden(_rms(x, g_ref[...]).astype(BF16), wg_ref, wu_ref, h_ref)
    o_ref[...] = x + _dot(h_ref[...], wd_ref[...])


def _resident(shape, index):
    return pl.BlockSpec(shape, index, pipeline_mode=pl.Buffered(1))


def _ffn(x, att, conv, wa, wc, g, wg, wu, wd):
    t, d = x.shape
    d_ff = wg.shape[1]
    tm = _tile(t, 512)
    return pl.pallas_call(
        _ffn_kernel,
        grid=(t // tm,),
        in_specs=_mixed_specs(tm, x, att, conv, wa, wc) + [
            pl.BlockSpec(g.shape, lambda i: (0, 0)),
            _resident(wg.shape, lambda i: (0, 0)), _resident(wu.shape, lambda i: (0, 0)),
            _resident(wd.shape, lambda i: (0, 0))],
        out_specs=pl.BlockSpec((tm, d), lambda i: (i, 0)),
        out_shape=jax.ShapeDtypeStruct((t, d), F32),
        scratch_shapes=[pltpu.VMEM((tm, d_ff), BF16)],
        compiler_params=_params("parallel"),
        name="ffn_dense",
    )(x, att, conv, wa, wc, g, wg, wu, wd)


def _expert_kernel(be_ref, x_ref, wg_ref, wu_ref, wd_ref, o_ref, h_ref):
    _swiglu_hidden(x_ref[...], wg_ref, wu_ref, h_ref)
    o_ref[...] = _dot(h_ref[...], wd_ref[...].astype(BF16))


def _expert_blocks(xg, block_e, wg, wu, wd):
    rows, d = xg.shape
    d_ff = wg.shape[2]
    grid_spec = pltpu.PrefetchScalarGridSpec(
        num_scalar_prefetch=1,
        grid=(rows // MOE_ROWS,),
        in_specs=[pl.BlockSpec((MOE_ROWS, d), lambda i, be: (i, 0)),
                  _resident((None, d, d_ff), lambda i, be: (be[i], 0, 0)),
                  _resident((None, d, d_ff), lambda i, be: (be[i], 0, 0)),
                  _resident((None, d_ff, d), lambda i, be: (be[i], 0, 0))],
        out_specs=pl.BlockSpec((MOE_ROWS, d), lambda i, be: (i, 0)),
        scratch_shapes=[pltpu.VMEM((MOE_ROWS, d_ff), BF16)],
    )
    return pl.pallas_call(
        _expert_kernel,
        grid_spec=grid_spec,
        out_shape=jax.ShapeDtypeStruct((rows, d), F32),
        compiler_params=_params("arbitrary"),
        name="moe_experts",
    )(block_e, xg, wg, wu, wd)


def _router_kernel(x_ref, att_ref, conv_ref, wa_ref, wc_ref, g_ref, r_ref, xo_ref, hn_ref, route_ref, *, n_experts):
    x = _mixed_residual(x_ref, att_ref, conv_ref, wa_ref, wc_ref)
    xo_ref[...] = x
    hn = _rms(x, g_ref[...])
    hn_ref[...] = hn.astype(BF16)
    h_hi, h_lo = _split_bf16(hn)
    r_hi, r_lo = _split_bf16(r_ref[...])
    logits = _dot(h_hi, r_hi) + (_dot(h_hi, r_lo) + _dot(h_lo, r_hi))
    lane = lax.broadcasted_iota(jnp.int32, logits.shape, 1).astype(F32)
    neg = jnp.float32(-jnp.inf)
    logits = jnp.where(lane < n_experts, logits, neg)
    v1 = jnp.max(logits, axis=-1, keepdims=True)
    i1 = jnp.min(jnp.where(logits == v1, lane, float(LANES)), axis=-1, keepdims=True)
    rest = jnp.where(lane == i1, neg, logits)
    v2 = jnp.max(rest, axis=-1, keepdims=True)
    i2 = jnp.min(jnp.where(rest == v2, lane, float(LANES)), axis=-1, keepdims=True)
    e = jnp.exp(v2 - v1)
    g1 = 1.0 / (1.0 + e)
    g2 = e / (1.0 + e)
    route_ref[...] = jnp.where(lane == 0, i1, jnp.where(lane == 1, i2, jnp.where(lane == 2, g1,
                               jnp.where(lane == 3, g2, 0.0))))


def _router(x, att, conv, wa, wc, g, router_pad, n_experts):
    t, d = x.shape
    tm = _tile(t, 512)
    row = lambda n: pl.BlockSpec((tm, n), lambda i: (i, 0))
    return pl.pallas_call(
        functools.partial(_router_kernel, n_experts=n_experts),
        grid=(t // tm,),
        in_specs=_mixed_specs(tm, x, att, conv, wa, wc) + [
            pl.BlockSpec(g.shape, lambda i: (0, 0)), pl.BlockSpec(router_pad.shape, lambda i: (0, 0))],
        out_specs=[row(d), row(d), row(LANES)],
        out_shape=[jax.ShapeDtypeStruct((t, d), F32), jax.ShapeDtypeStruct((t, d), BF16),
                   jax.ShapeDtypeStruct((t, LANES), F32)],
        compiler_params=_params("parallel"),
        name="router",
    )(x, att, conv, wa, wc, g, router_pad)


def _moe_dispatch(x, att, conv, wa, wc, g, router):
    t, d = x.shape
    n_experts = router.shape[1]
    router_pad = jnp.pad(router, ((0, 0), (0, LANES - n_experts)))
    x, hn, route = _router(x, att, conv, wa, wc, g, router_pad, n_experts)
    flat_e = route[:, :TOP_K].astype(jnp.int32).reshape(-1)
    n_assign = t * TOP_K
    onehot = (flat_e[:, None] == jnp.arange(n_experts, dtype=jnp.int32)[None, :]).astype(jnp.int32)
    before = jnp.cumsum(onehot, axis=0) - onehot
    counts = jnp.sum(onehot, axis=0)
    padded = (counts + MOE_ROWS - 1) // MOE_ROWS * MOE_ROWS
    pad_end = jnp.cumsum(padded)
    pad_start = pad_end - padded
    dest = jnp.sum(onehot * (before + pad_start[None, :]), axis=1)
    n_blocks = -(-n_assign // MOE_ROWS) + n_experts
    rows = n_blocks * MOE_ROWS
    row_tok = jnp.zeros((rows,), jnp.int32).at[dest].set(
        jnp.arange(n_assign, dtype=jnp.int32) // TOP_K, unique_indices=True, mode='promise_in_bounds')
    block_start = jnp.arange(n_blocks, dtype=jnp.int32) * MOE_ROWS
    block_e = jnp.minimum(jnp.sum((pad_end[None, :] <= block_start[:, None]).astype(jnp.int32), axis=1),
                          n_experts - 1)
    return x, (hn[row_tok], block_e, dest.reshape(t, TOP_K), route)


def _moe_experts(xg, block_e, pos, route, wg, wu, wd):
    y = _expert_blocks(xg, block_e, wg, wu, wd)
    return [y[pos[:, 0]], y[pos[:, 1]], route]


def _ple_kernel(*refs, n_add, final):
    x_ref = refs[0]
    p_ref, g_ref, wgate_ref, wproj_ref = refs[1 + n_add:5 + n_add]
    fg_ref = refs[5 + n_add] if final else None
    o_ref = refs[-1]
    x = x_ref[...]
    if n_add:
        ya_ref, yb_ref, route_ref = refs[1:4]
        route = route_ref[...]
        x = x + (ya_ref[...] * route[:, TOP_K:TOP_K + 1] + yb_ref[...] * route[:, TOP_K + 1:TOP_K + 2])
    gate = _sigmoid(_dot(_rms(x, g_ref[...]).astype(BF16), wgate_ref[...]))
    x = x + gate * _dot(p_ref[...].astype(BF16), wproj_ref[...])
    o_ref[...] = _rms(x, fg_ref[...]) if final else x


def _ple(x, adds, p, g, wgate, wproj, final_g):
    t, d = x.shape
    tm = _tile(t, 512)
    row = lambda a: pl.BlockSpec((tm, a.shape[1]), lambda i: (i, 0))
    full = lambda a: pl.BlockSpec(a.shape, lambda i: (0, 0))
    final = final_g is not None
    ops = [x, *adds, p, g, wgate, wproj] + ([final_g] if final else [])
    specs = [row(x)] + [row(a) for a in adds] + [row(p), full(g), full(wgate), full(wproj)]
    specs += [full(final_g)] if final else []
    return pl.pallas_call(
        functools.partial(_ple_kernel, n_add=len(adds), final=final),
        grid=(t // tm,),
        in_specs=specs,
        out_specs=row(x),
        out_shape=jax.ShapeDtypeStruct((t, d), F32),
        compiler_params=_params("parallel"),
        name="ple",
    )(*ops)


def kernel(x_prompt, x_sample, cache_k, cache_v, state_conv, page_table, p_prompt, p_sample,
           w_in, sb_bias, w_out, conv_w, conv_b, conv_norm_g, conv_norm_b, norm_mix_g, norm_ffn_g,
           norm_ple_g, w_ple_gate, w_ple_proj, ffn_w_gate, ffn_w_up, ffn_w_down,
           moe_router, moe_w_gate, moe_w_up, moe_w_down, final_norm_g):
    depth, d_model = norm_mix_g.shape
    n_heads = sb_bias.shape[1]
    att = n_heads * HEAD_DIM
    conv = conv_w.shape[2]
    taps = conv_w.shape[1]
    n_pool, page = cache_k.shape[1], cache_k.shape[2]
    cache_kt = jnp.transpose(cache_k, (0, 1, 3, 4, 2)).reshape(depth, n_pool, att, page)
    cache_vt = jnp.transpose(cache_v, (0, 1, 3, 4, 2)).reshape(depth, n_pool, att, page)
    vec = lambda a: a.reshape(1, -1)
    final_g = vec(final_norm_g)
    bsz, seq, _ = x_prompt.shape

    def mix_in(i, x, a, u3, ctx):
        b, s, _ = u3.shape
        c3 = _conv_module(ctx, u3, conv_w[i], vec(conv_b[i]), vec(conv_norm_g[i]), vec(conv_norm_b[i]))
        wo = w_out[i].astype(BF16)
        mixed = (a, c3.reshape(b * s, conv), wo[:att], wo[att:])
        routed = None
        if i % 2:
            x, routed = _moe_dispatch(x, *mixed, vec(norm_ffn_g[i]), moe_router[i // 2])
        return x, mixed, routed, jnp.concatenate([ctx, u3], axis=1)[:, -(taps - 1):]

    def mix_out(i, x, mixed, routed, p3):
        j = i // 2
        if i % 2 == 0:
            x = _ffn(x, *mixed, vec(norm_ffn_g[i]), ffn_w_gate[j].astype(BF16), ffn_w_up[j].astype(BF16),
                     ffn_w_down[j].astype(BF16))
            adds = []
        else:
            adds = _moe_experts(*routed, moe_w_gate[j], moe_w_up[j], moe_w_down[j])
        return _ple(x, adds, p3.reshape(x.shape[0], -1), vec(norm_ple_g[i]), w_ple_gate[i].astype(BF16),
                    w_ple_proj[i].astype(BF16), final_g if i == depth - 1 else None)

    xp = x_prompt.reshape(bsz * seq, d_model)
    db, n_q, _ = x_sample.shape
    xs = x_sample.reshape(db * n_q, d_model)
    conv_zero = jnp.zeros((bsz, taps - 1, conv), F32)
    kv_t = None
    cp_l, ks_l, vs_l, cs_l = [], [], [], []
    for i in range(depth):
        g = vec(norm_mix_g[i])
        w = w_in[i].astype(BF16)
        wkv_t = w_in[i][:, att:3 * att].T.astype(BF16)
        q, k_t, v_t, u = _inproj_kv_t(xp, g, w, wkv_t, att, conv, bsz, kv_t)
        kv_t = (k_t, v_t)
        a = _sb_prompt(q.reshape(bsz, seq, att), k_t, v_t, sb_bias[i]).reshape(bsz * seq, att)
        xp, mixed_p, routed_p, cp = mix_in(i, xp, a, u.reshape(bsz, seq, conv), conv_zero)

        q, k, v, u = _inproj(xs, g, w, att, conv)
        q3, k3, v3 = (t.reshape(db, n_q, att) for t in (q, k, v))
        a = _sb_paged(q3, k3, v3, cache_kt, cache_vt, i, page_table, sb_bias[i]).reshape(db * n_q, att)
        xs, mixed_s, routed_s, cs = mix_in(i, xs, a, u.reshape(db, n_q, conv), state_conv[i])

        xp = mix_out(i, xp, mixed_p, routed_p, p_prompt[i])
        xs = mix_out(i, xs, mixed_s, routed_s, p_sample[i])
        cp_l.append(cp)
        ks_l.append(k3.reshape(db, n_q, n_heads, HEAD_DIM))
        vs_l.append(v3.reshape(db, n_q, n_heads, HEAD_DIM))
        cs_l.append(cs)

    heads_last = lambda t: jnp.transpose(t.reshape(depth, bsz, n_heads, HEAD_DIM, seq), (0, 1, 4, 2, 3))
    return (xp.reshape(bsz, seq, d_model), xs.reshape(db, n_q, d_model), heads_last(kv_t[0]), heads_last(kv_t[1]),
            jnp.stack(cp_l), jnp.stack(ks_l), jnp.stack(vs_l), jnp.stack(cs_l))
```

```python
import functools
import math

import jax
import jax.numpy as jnp
from jax import lax
from jax.experimental import pallas as pl
from jax.experimental.pallas import tpu as pltpu

F32 = jnp.float32
BF16 = jnp.bfloat16
MX = jnp.bfloat16
EPS = 1e-6
HEAD_DIM = 64
LANES = 128
TOP_K = 2
VMEM_LIMIT = 56 * 1024 * 1024
PAGES_PER_STEP = 16
MOE_ROWS = 256
FF_CHUNK = 256


def _params(*sem):
    return pltpu.CompilerParams(dimension_semantics=sem, vmem_limit_bytes=VMEM_LIMIT)


def _tile(n, pref):
    if n <= pref:
        return n
    t = pref - pref % 8
    while n % t:
        t -= 8
    return t


def _pallas(body, *, out_shape, flops, transcendentals=0, bytes_accessed=None, **kw):
    def run(*args):
        outs = out_shape if isinstance(out_shape, (list, tuple)) else [out_shape]
        nbytes = bytes_accessed
        if nbytes is None:
            nbytes = sum(math.prod(a.shape) * jnp.dtype(a.dtype).itemsize for a in (*args, *outs))
        cost = pl.CostEstimate(flops=int(flops), transcendentals=int(transcendentals), bytes_accessed=int(nbytes))
        return pl.pallas_call(body, out_shape=out_shape, cost_estimate=cost, **kw)(*args)
    return run


def _dot(a, b):
    return jnp.dot(a, b, preferred_element_type=F32)


def _dot_nt(a, b):
    return lax.dot_general(a, b, (((1,), (1,)), ((), ())), preferred_element_type=F32)


def _rms(xf, g):
    return xf * lax.rsqrt(jnp.mean(xf * xf, axis=-1, keepdims=True) + EPS) * g


def _sigmoid(x):
    return 1.0 / (1.0 + jnp.exp(-x))


def _split_bf16(x):
    hi = x.astype(BF16)
    lo = (x - hi.astype(F32)).astype(BF16)
    return hi, lo


def _inproj_kernel(*refs, att, conv, kv_t, n_prev=0):
    x_ref, g_ref, w_ref = refs[:3]
    q_ref, k_ref, v_ref, u_ref = refs[-4:]
    hn = _rms(x_ref[...], g_ref[...]).astype(BF16)
    col = lambda c0, n: _dot(hn, w_ref[:, c0:c0 + n])
    q_ref[...] = col(0, att)
    if kv_t:
        kv = _dot_nt(refs[3][...], hn)
        k_ref[n_prev] = kv[:att]
        v_ref[n_prev] = kv[att:]
        if n_prev:
            k_ref[0:n_prev] = refs[4][...]
            v_ref[0:n_prev] = refs[5][...]
    else:
        k_ref[...] = col(att, att)
        v_ref[...] = col(2 * att, att)
    a = col(3 * att, conv)
    g = col(3 * att + conv, conv)
    u_ref[...] = a * _sigmoid(g)


def _inproj(x, g, w, att, conv):
    t, d = x.shape
    tm = _tile(t, 512)
    row = lambda n: pl.BlockSpec((tm, n), lambda i: (i, 0))
    full = lambda a: pl.BlockSpec(a.shape, lambda i: (0, 0))
    return _pallas(
        functools.partial(_inproj_kernel, att=att, conv=conv, kv_t=False),
        flops=2 * t * d * w.shape[1], transcendentals=t * conv,
        grid=(t // tm,),
        in_specs=[row(d), full(g), full(w)],
        out_specs=[row(att), row(att), row(att), row(conv)],
        out_shape=[jax.ShapeDtypeStruct((t, att), F32)] * 3 + [jax.ShapeDtypeStruct((t, conv), F32)],
        compiler_params=_params("parallel"),
        name="inproj",
    )(x, g, w)


def _inproj_kv_t(x, g, w, wkv_t, att, conv, batch, kv_prev):
    t, d = x.shape
    seq = t // batch
    tm = _tile(seq, 512)
    per_seq = seq // tm
    prev = list(kv_prev) if kv_prev is not None else []
    n_prev = prev[0].shape[0] if prev else 0
    row = lambda n: pl.BlockSpec((tm, n), lambda i: (i, 0))
    full = lambda a: pl.BlockSpec(a.shape, lambda i: (0, 0))
    slabs = lambda n: pl.BlockSpec((n, None, att, tm), lambda i: (0, i // per_seq, 0, i % per_seq))
    kv_shape = jax.ShapeDtypeStruct((n_prev + 1, batch, att, seq), F32)
    return _pallas(
        functools.partial(_inproj_kernel, att=att, conv=conv, kv_t=True, n_prev=n_prev),
        flops=2 * t * d * (w.shape[1] - 2 * att + wkv_t.shape[0]), transcendentals=t * conv,
        grid=(t // tm,),
        in_specs=[row(d), full(g), full(w), full(wkv_t)] + [slabs(n_prev)] * len(prev),
        out_specs=[row(att), slabs(n_prev + 1), slabs(n_prev + 1), row(conv)],
        out_shape=[jax.ShapeDtypeStruct((t, att), F32), kv_shape, kv_shape, jax.ShapeDtypeStruct((t, conv), F32)],
        compiler_params=_params("parallel"),
        name="inproj_kv_t",
    )(x, g, w, wkv_t, *prev)


def _log_sigmoids(z):
    ls = jnp.minimum(z, 0.0) - jnp.log(1.0 + jnp.exp(-jnp.abs(z)))
    return ls, ls - z


def _later_than(n):
    j = lax.broadcasted_iota(jnp.int32, (n, n), 0)
    s = lax.broadcasted_iota(jnp.int32, (n, n), 1)
    return jnp.where(j > s, 1.0, 0.0).astype(MX)


def _sb_weights(z, tri, carry, mask):
    ls, lk = _log_sigmoids(z)
    if mask is not None:
        lk = jnp.where(mask, lk, 0.0)
    later = _dot(lk.astype(MX), tri)
    a = jnp.exp(ls + later + carry)
    if mask is not None:
        a = jnp.where(mask, a, 0.0)
    return a, carry + later[:, 0:1] + lk[:, 0:1]


def _sb_prompt_kernel(bias_ref, q_ref, k_ref, v_ref, o_ref, z_ref, a_ref, carry_ref, *, tq):
    hp = pl.program_id(1)
    qi = pl.program_id(2)
    first = lax.broadcasted_iota(jnp.int32, (tq, LANES), 1) < HEAD_DIM
    q = q_ref[...] * (HEAD_DIM ** -0.5)
    q2 = jnp.concatenate([jnp.where(first, q, 0.0), jnp.where(first, 0.0, q)], axis=0).astype(MX)
    bias_a = bias_ref[2 * hp]
    bias_b = bias_ref[2 * hp + 1]
    tri = _later_than(tq)
    r = lax.broadcasted_iota(jnp.int32, (2 * tq, tq), 0)
    c = lax.broadcasted_iota(jnp.int32, (2 * tq, tq), 1)
    causal = c < jnp.where(r < tq, r, r - tq)
    v_first = lax.broadcasted_iota(jnp.int32, (LANES, tq), 0) < HEAD_DIM

    def cols(j):
        return pl.ds(pl.multiple_of(j * tq, tq), tq)

    def logits(j):
        z = _dot(q2, k_ref[:, cols(j)].astype(MX))
        return jnp.concatenate([z[:tq] + bias_a, z[tq:] + bias_b], axis=0)

    def weighted_values(a2, j):
        vb = v_ref[:, cols(j)]
        v2 = jnp.concatenate([jnp.where(v_first, vb, 0.0), jnp.where(v_first, 0.0, vb)], axis=1).astype(MX)
        return _dot_nt(a2, v2)

    def keep(a, carry):
        a_ref[...] = jnp.concatenate([a[:tq], a[tq:]], axis=1).astype(MX)
        carry_ref[...] = carry

    keep(*_sb_weights(logits(qi), tri, jnp.zeros((2 * tq, 1), F32), causal))
    z_ref[...] = logits(jnp.maximum(qi - 1, 0))
    o_ref[...] = jnp.zeros((tq, LANES), F32)

    def body(n, c):
        j = qi - 1 - n
        o_ref[...] += weighted_values(a_ref[...], j + 1)
        z = z_ref[...]
        z_ref[...] = logits(jnp.maximum(j - 1, 0))
        keep(*_sb_weights(z, tri, carry_ref[...], None))
        return c

    lax.fori_loop(0, qi, body, 0)
    o_ref[...] += weighted_values(a_ref[...], 0)


def _sb_prompt(q, k_t, v_t, bias):
    b, s, w = q.shape
    tq = _tile(s, 256)
    layer = k_t.shape[0] - 1
    qspec = pl.BlockSpec((None, tq, LANES), lambda bi, hp, qi: (bi, qi, hp))
    kvspec = pl.BlockSpec((None, None, LANES, s), lambda bi, hp, qi: (layer, bi, hp, 0))
    tiles = b * (w // LANES) * (s // tq) * (s // tq + 1) // 2
    return _pallas(
        functools.partial(_sb_prompt_kernel, tq=tq),
        flops=tiles * (8 * tq * tq * LANES + 4 * tq * tq * tq), transcendentals=tiles * 6 * tq * tq,
        bytes_accessed=4 * b * s * w * 4,
        grid=(b, w // LANES, s // tq),
        in_specs=[pl.BlockSpec(memory_space=pltpu.SMEM), qspec, kvspec, kvspec],
        out_specs=qspec,
        out_shape=jax.ShapeDtypeStruct((b, s, w), F32),
        scratch_shapes=[pltpu.VMEM((2 * tq, tq), F32), pltpu.VMEM((tq, 2 * tq), MX),
                        pltpu.VMEM((2 * tq, 1), F32)],
        compiler_params=_params("parallel", "parallel", "arbitrary"),
        name="sb_prompt",
    )(bias, q, k_t, v_t)


def _sb_paged_kernel(pt_ref, bias_ref, q_ref, kn_ref, vn_ref, *rest, n_heads, n_q, page):
    kpages = rest[:PAGES_PER_STEP]
    vpages = rest[PAGES_PER_STEP:2 * PAGES_PER_STEP]
    o_ref, qbd_ref, acc_ref, new_ref, carry_ref, w_ref = rest[2 * PAGES_PER_STEP:]
    step = pl.program_id(1)
    rows = n_q * n_heads
    width = n_heads * HEAD_DIM
    pairs = PAGES_PER_STEP // 2
    bias = bias_ref[...]
    tri = _later_than(2 * page)

    @pl.when(step == 0)
    def _init():
        q = q_ref[...] * (HEAD_DIM ** -0.5)
        rep = jnp.concatenate([jnp.broadcast_to(q[t:t + 1, :], (n_heads, width)) for t in range(n_q)], axis=0)
        row = lax.broadcasted_iota(jnp.int32, (rows, width), 0)
        lane = lax.broadcasted_iota(jnp.int32, (rows, width), 1)
        qbd = jnp.where(lane // HEAD_DIM == row % n_heads, rep, 0.0)
        qbd_ref[...] = qbd
        pad = jnp.zeros((page - n_q, width), F32)
        kn = jnp.concatenate([kn_ref[...], pad], axis=0)
        vn = jnp.concatenate([vn_ref[...], pad], axis=0)
        key = lax.broadcasted_iota(jnp.int32, (rows, page), 1)
        t = lax.broadcasted_iota(jnp.int32, (rows, page), 0) // n_heads
        z = _dot_nt(qbd.astype(MX), kn.astype(MX)) + bias
        a, carry = _sb_weights(z, tri[:page, :page], jnp.zeros((rows, 1), F32), key < t)
        new_ref[...] = _dot(a.astype(MX), vn.astype(MX))
        carry_ref[...] = carry
        acc_ref[...] = jnp.zeros(acc_ref.shape, F32)
        w_ref[...] = jnp.zeros(w_ref.shape, MX)

    w_prev = w_ref[...]
    qbd = qbd_ref[...].astype(MX)
    scores = []
    for i in range(pairs):
        keys = jnp.concatenate([kpages[2 * i][...], kpages[2 * i + 1][...]], axis=1)
        scores.append(_log_sigmoids(_dot(qbd, keys.astype(MX)) + bias))
    later = _dot(jnp.concatenate([lk for _, lk in scores], axis=0).astype(MX), tri)
    carry = carry_ref[...]
    weights = [None] * pairs
    for i in reversed(range(pairs)):
        ls, lk = scores[i]
        lt = later[i * rows:(i + 1) * rows]
        weights[i] = jnp.exp(ls + lt + carry)
        carry = carry + lt[:, 0:1] + lk[:, 0:1]
    carry_ref[...] = carry
    w_ref[...] = jnp.concatenate(weights, axis=1).astype(MX)
    values = jnp.concatenate([vpages[p][...] for p in range(PAGES_PER_STEP)], axis=1)
    acc_ref[...] += _dot_nt(values.astype(MX), w_prev)

    @pl.when(step == pl.num_programs(1) - 1)
    def _finish():
        r = lax.broadcasted_iota(jnp.int32, (width, rows), 0)
        c = lax.broadcasted_iota(jnp.int32, (width, rows), 1)
        hi, lo = _split_bf16(jnp.where(c % n_heads == r // HEAD_DIM, acc_ref[...], 0.0))
        t = lax.broadcasted_iota(jnp.int32, (8, rows), 0)
        cc = lax.broadcasted_iota(jnp.int32, (8, rows), 1)
        pick = jnp.where(cc // n_heads == t, 1.0, 0.0).astype(BF16)
        past = _dot_nt(pick, hi) + _dot_nt(pick, lo)
        row = lax.broadcasted_iota(jnp.int32, (rows, width), 0)
        lane = lax.broadcasted_iota(jnp.int32, (rows, width), 1)
        own = jnp.where(lane // HEAD_DIM == row % n_heads, new_ref[...], 0.0)
        o_ref[...] = past[:n_q] + jnp.sum(own.reshape(n_q, n_heads, width), axis=1)


def _sb_paged(q, k_new, v_new, cache_kt, cache_vt, layer, page_table, bias):
    db, n_q, width = q.shape
    n_heads = width // HEAD_DIM
    page = cache_kt.shape[3]
    n_pages = page_table.shape[1]
    steps = n_pages // PAGES_PER_STEP
    rows = n_q * n_heads
    bias_col = jnp.tile(bias, n_q).reshape(rows, 1)

    def page_spec(p, delay):
        def index(bi, si, pt):
            group = steps - 1 - jnp.clip(si - delay, 0, steps - 1)
            return (layer, pt[bi, group * PAGES_PER_STEP + p], 0, 0)
        return pl.BlockSpec((None, None, width, page), index)

    small = pl.BlockSpec((None, n_q, width), lambda bi, si, pt: (bi, 0, 0))
    kpages = [page_spec(p, 0) for p in range(PAGES_PER_STEP)]
    vpages = [page_spec(p, 1) for p in range(PAGES_PER_STEP)]
    grid_spec = pltpu.PrefetchScalarGridSpec(
        num_scalar_prefetch=1,
        grid=(db, steps + 1),
        in_specs=[pl.BlockSpec((rows, 1), lambda bi, si, pt: (0, 0)), small, small, small] + kpages + vpages,
        out_specs=small,
        scratch_shapes=[pltpu.VMEM((rows, width), F32), pltpu.VMEM((width, rows), F32),
                        pltpu.VMEM((rows, width), F32), pltpu.VMEM((rows, 1), F32),
                        pltpu.VMEM((rows, PAGES_PER_STEP * page), MX)],
    )
    keys = n_pages * page
    return _pallas(
        functools.partial(_sb_paged_kernel, n_heads=n_heads, n_q=n_q, page=page),
        flops=db * keys * rows * (4 * width + 4 * page), transcendentals=db * keys * rows * 3,
        bytes_accessed=2 * db * keys * width * 4,
        grid_spec=grid_spec,
        out_shape=jax.ShapeDtypeStruct((db, n_q, width), F32),
        compiler_params=_params("parallel", "arbitrary"),
        name="sb_paged",
    )(page_table, bias_col, q, k_new, v_new, *([cache_kt] * PAGES_PER_STEP), *([cache_vt] * PAGES_PER_STEP))


CONV_PAD = 32


def _conv_kernel(ctx_ref, u_ref, w_ref, b_ref, g_ref, nb_ref, o_ref, ext_ref, *, taps, seq, rows):
    n_ctx = taps - 1
    off = CONV_PAD - n_ctx
    ext_ref[0:CONV_PAD, :] = jnp.zeros((CONV_PAD, ext_ref.shape[1]), F32)
    ext_ref[off:CONV_PAD, :] = ctx_ref[...]
    if seq % 8:
        ext_ref[CONV_PAD:, :] = jnp.zeros((ext_ref.shape[0] - CONV_PAD, ext_ref.shape[1]), F32)
    ext_ref[CONV_PAD:CONV_PAD + seq, :] = u_ref[...]
    w = w_ref[...]
    out_rows = min(rows, seq)

    def chunk(i, _):
        base = pl.multiple_of(i * rows, rows)
        span = rows + CONV_PAD
        parts = []
        for c0 in range(0, ext_ref.shape[1], LANES):
            win = ext_ref[pl.ds(base, span), c0:c0 + LANES]
            part = jnp.zeros((rows, LANES), F32) + b_ref[:, c0:c0 + LANES]
            for s in range(8):
                shifted = win if s == 0 else pltpu.roll(win, span - s, axis=0)
                for k in range(taps):
                    o = off + k - s
                    if o % 8 == 0:
                        part = part + w[k:k + 1, c0:c0 + LANES] * shifted[o:o + rows]
            parts.append(part)
        acc = jnp.concatenate(parts, axis=1)
        mu = jnp.mean(acc, axis=-1, keepdims=True)
        xc = acc - mu
        y = xc * lax.rsqrt(jnp.mean(xc * xc, axis=-1, keepdims=True) + EPS) * g_ref[...] + nb_ref[...]
        y = y * _sigmoid(y)
        o_ref[pl.ds(base, out_rows), :] = y[:out_rows]
        return 0

    lax.fori_loop(0, max(seq // rows, 1), chunk, 0)


def _conv_module(ctx, u, w, b, g, nb):
    bsz, seq, ch = u.shape
    taps = w.shape[0]
    rows = 128 if seq >= 128 else 8
    ext_rows = CONV_PAD + max(seq, rows)
    vec = lambda a: pl.BlockSpec(a.shape, lambda i: (0, 0))
    return _pallas(
        functools.partial(_conv_kernel, taps=taps, seq=seq, rows=rows),
        flops=2 * taps * bsz * seq * ch, transcendentals=bsz * seq * ch,
        grid=(bsz,),
        in_specs=[pl.BlockSpec((None, taps - 1, ch), lambda i: (i, 0, 0)),
                  pl.BlockSpec((None, seq, ch), lambda i: (i, 0, 0)),
                  vec(w), vec(b), vec(g), vec(nb)],
        out_specs=pl.BlockSpec((None, seq, ch), lambda i: (i, 0, 0)),
        out_shape=jax.ShapeDtypeStruct((bsz, seq, ch), F32),
        scratch_shapes=[pltpu.VMEM((ext_rows, ch), F32)],
        compiler_params=_params("parallel"),
        name="conv_module",
    )(ctx, u, w, b, g, nb)


def _mixed_residual(x_ref, att_ref, conv_ref, wa_ref, wc_ref):
    return (x_ref[...] + _dot(att_ref[...].astype(BF16), wa_ref[...])
            + _dot(conv_ref[...].astype(BF16), wc_ref[...]))


def _mixed_specs(tm, x, att, conv, wa, wc):
    row = lambda a: pl.BlockSpec((tm, a.shape[1]), lambda i: (i, 0))
    full = lambda a: pl.BlockSpec(a.shape, lambda i: (0, 0))
    return [row(x), row(att), row(conv), full(wa), full(wc)]


def _swiglu_hidden(hn, wg_ref, wu_ref, h_ref):
    d_ff = h_ref.shape[1]
    for c0 in range(0, d_ff, FF_CHUNK):
        g = _dot(hn, wg_ref[:, c0:c0 + FF_CHUNK].astype(BF16))
        u = _dot(hn, wu_ref[:, c0:c0 + FF_CHUNK].astype(BF16))
        h_ref[:, c0:c0 + FF_CHUNK] = (g * _sigmoid(g) * u).astype(BF16)


def _ffn_kernel(x_ref, att_ref, conv_ref, wa_ref, wc_ref, g_ref, wg_ref, wu_ref, wd_ref, o_ref, h_ref):
    x = _mixed_residual(x_ref, att_ref, conv_ref, wa_ref, wc_ref)
    _swiglu_hidden(_rms(x, g_ref[...]).astype(BF16), wg_ref, wu_ref, h_ref)
    o_ref[...] = x + _dot(h_ref[...], wd_ref[...])


def _resident(shape, index):
    return pl.BlockSpec(shape, index, pipeline_mode=pl.Buffered(1))


def _ffn(x, att, conv, wa, wc, g, wg, wu, wd):
    t, d = x.shape
    d_ff = wg.shape[1]
    tm = _tile(t, 512)
    return _pallas(
        _ffn_kernel,
        flops=2 * t * d * d + 6 * t * d * d_ff, transcendentals=t * d_ff,
        grid=(t // tm,),
        in_specs=_mixed_specs(tm, x, att, conv, wa, wc) + [
            pl.BlockSpec(g.shape, lambda i: (0, 0)),
            _resident(wg.shape, lambda i: (0, 0)), _resident(wu.shape, lambda i: (0, 0)),
            _resident(wd.shape, lambda i: (0, 0))],
        out_specs=pl.BlockSpec((tm, d), lambda i: (i, 0)),
        out_shape=jax.ShapeDtypeStruct((t, d), F32),
        scratch_shapes=[pltpu.VMEM((tm, d_ff), BF16)],
        compiler_params=_params("parallel"),
        name="ffn_dense",
    )(x, att, conv, wa, wc, g, wg, wu, wd)


def _expert_kernel(be_ref, x_ref, wg_ref, wu_ref, wd_ref, after_ref, o_ref, h_ref):
    _swiglu_hidden(x_ref[...], wg_ref, wu_ref, h_ref)
    o_ref[...] = _dot(h_ref[...], wd_ref[...].astype(BF16))


def _expert_blocks(xg, block_e, wg, wu, wd, after):
    rows, d = xg.shape
    d_ff = wg.shape[2]
    grid_spec = pltpu.PrefetchScalarGridSpec(
        num_scalar_prefetch=1,
        grid=(rows // MOE_ROWS,),
        in_specs=[pl.BlockSpec((MOE_ROWS, d), lambda i, be: (i, 0)),
                  _resident((None, d, d_ff), lambda i, be: (be[i], 0, 0)),
                  _resident((None, d, d_ff), lambda i, be: (be[i], 0, 0)),
                  _resident((None, d_ff, d), lambda i, be: (be[i], 0, 0)),
                  pl.BlockSpec((8, LANES), lambda i, be: (0, 0))],
        out_specs=pl.BlockSpec((MOE_ROWS, d), lambda i, be: (i, 0)),
        scratch_shapes=[pltpu.VMEM((MOE_ROWS, d_ff), BF16)],
    )
    return _pallas(
        _expert_kernel,
        flops=6 * rows * d * d_ff, transcendentals=rows * d_ff,
        grid_spec=grid_spec,
        out_shape=jax.ShapeDtypeStruct((rows, d), F32),
        compiler_params=_params("arbitrary"),
        name="moe_experts",
    )(block_e, xg, wg, wu, wd, after)


def _router_kernel(x_ref, att_ref, conv_ref, wa_ref, wc_ref, g_ref, r_ref, xo_ref, hn_ref, route_ref, *, n_experts):
    x = _mixed_residual(x_ref, att_ref, conv_ref, wa_ref, wc_ref)
    xo_ref[...] = x
    hn = _rms(x, g_ref[...])
    hn_ref[...] = hn.astype(BF16)
    h_hi, h_lo = _split_bf16(hn)
    r_hi, r_lo = _split_bf16(r_ref[...])
    logits = _dot(h_hi, r_hi) + (_dot(h_hi, r_lo) + _dot(h_lo, r_hi))
    lane = lax.broadcasted_iota(jnp.int32, logits.shape, 1).astype(F32)
    neg = jnp.float32(-jnp.inf)
    logits = jnp.where(lane < n_experts, logits, neg)
    v1 = jnp.max(logits, axis=-1, keepdims=True)
    i1 = jnp.min(jnp.where(logits == v1, lane, float(LANES)), axis=-1, keepdims=True)
    rest = jnp.where(lane == i1, neg, logits)
    v2 = jnp.max(rest, axis=-1, keepdims=True)
    i2 = jnp.min(jnp.where(rest == v2, lane, float(LANES)), axis=-1, keepdims=True)
    e = jnp.exp(v2 - v1)
    g1 = 1.0 / (1.0 + e)
    g2 = e / (1.0 + e)
    route_ref[...] = jnp.where(lane == 0, i1, jnp.where(lane == 1, i2, jnp.where(lane == 2, g1,
                               jnp.where(lane == 3, g2, 0.0))))


def _router(x, att, conv, wa, wc, g, router_pad, n_experts):
    t, d = x.shape
    tm = _tile(t, 512)
    row = lambda n: pl.BlockSpec((tm, n), lambda i: (i, 0))
    return _pallas(
        functools.partial(_router_kernel, n_experts=n_experts),
        flops=2 * t * d * d + 6 * t * d * LANES, transcendentals=t,
        grid=(t // tm,),
        in_specs=_mixed_specs(tm, x, att, conv, wa, wc) + [
            pl.BlockSpec(g.shape, lambda i: (0, 0)), pl.BlockSpec(router_pad.shape, lambda i: (0, 0))],
        out_specs=[row(d), row(d), row(LANES)],
        out_shape=[jax.ShapeDtypeStruct((t, d), F32), jax.ShapeDtypeStruct((t, d), BF16),
                   jax.ShapeDtypeStruct((t, LANES), F32)],
        compiler_params=_params("parallel"),
        name="router",
    )(x, att, conv, wa, wc, g, router_pad)


def _moe_dispatch(x, att, conv, wa, wc, g, router):
    t, d = x.shape
    n_experts = router.shape[1]
    router_pad = jnp.pad(router, ((0, 0), (0, LANES - n_experts)))
    x, hn, route = _router(x, att, conv, wa, wc, g, router_pad, n_experts)
    flat_e = route[:, :TOP_K].astype(jnp.int32).reshape(-1)
    n_assign = t * TOP_K
    onehot = (flat_e[:, None] == jnp.arange(n_experts, dtype=jnp.int32)[None, :]).astype(jnp.int32)
    before = jnp.cumsum(onehot, axis=0) - onehot
    counts = jnp.sum(onehot, axis=0)
    padded = (counts + MOE_ROWS - 1) // MOE_ROWS * MOE_ROWS
    pad_end = jnp.cumsum(padded)
    pad_start = pad_end - padded
    dest = jnp.sum(onehot * (before + pad_start[None, :]), axis=1)
    n_blocks = -(-n_assign // MOE_ROWS) + n_experts
    rows = n_blocks * MOE_ROWS
    row_tok = jnp.zeros((rows,), jnp.int32).at[dest].set(
        jnp.arange(n_assign, dtype=jnp.int32) // TOP_K, unique_indices=True, mode='promise_in_bounds')
    block_start = jnp.arange(n_blocks, dtype=jnp.int32) * MOE_ROWS
    block_e = jnp.minimum(jnp.sum((pad_end[None, :] <= block_start[:, None]).astype(jnp.int32), axis=1),
                          n_experts - 1)
    return x, (hn[row_tok], block_e, dest.reshape(t, TOP_K), route)


def _moe_experts(xg, block_e, pos, route, wg, wu, wd, after):
    y = _expert_blocks(xg, block_e, wg, wu, wd, after)
    return [y[pos[:, 0]], y[pos[:, 1]], route]


def _ple_kernel(*refs, n_add, final):
    x_ref = refs[0]
    p_ref, g_ref, wgate_ref, wproj_ref = refs[1 + n_add:5 + n_add]
    fg_ref = refs[5 + n_add] if final else None
    o_ref = refs[-1]
    x = x_ref[...]
    if n_add:
        ya_ref, yb_ref, route_ref = refs[1:4]
        route = route_ref[...]
        x = x + (ya_ref[...] * route[:, TOP_K:TOP_K + 1] + yb_ref[...] * route[:, TOP_K + 1:TOP_K + 2])
    gate = _sigmoid(_dot(_rms(x, g_ref[...]).astype(BF16), wgate_ref[...]))
    x = x + gate * _dot(p_ref[...].astype(BF16), wproj_ref[...])
    o_ref[...] = _rms(x, fg_ref[...]) if final else x


def _ple(x, adds, p, g, wgate, wproj, final_g):
    t, d = x.shape
    tm = _tile(t, 512)
    row = lambda a: pl.BlockSpec((tm, a.shape[1]), lambda i: (i, 0))
    full = lambda a: pl.BlockSpec(a.shape, lambda i: (0, 0))
    final = final_g is not None
    ops = [x, *adds, p, g, wgate, wproj] + ([final_g] if final else [])
    specs = [row(x)] + [row(a) for a in adds] + [row(p), full(g), full(wgate), full(wproj)]
    specs += [full(final_g)] if final else []
    return _pallas(
        functools.partial(_ple_kernel, n_add=len(adds), final=final),
        flops=2 * t * d * d + 2 * t * p.shape[1] * d, transcendentals=t * d,
        grid=(t // tm,),
        in_specs=specs,
        out_specs=row(x),
        out_shape=jax.ShapeDtypeStruct((t, d), F32),
        compiler_params=_params("parallel"),
        name="ple",
    )(*ops)


def kernel(x_prompt, x_sample, cache_k, cache_v, state_conv, page_table, p_prompt, p_sample,
           w_in, sb_bias, w_out, conv_w, conv_b, conv_norm_g, conv_norm_b, norm_mix_g, norm_ffn_g,
           norm_ple_g, w_ple_gate, w_ple_proj, ffn_w_gate, ffn_w_up, ffn_w_down,
           moe_router, moe_w_gate, moe_w_up, moe_w_down, final_norm_g):
    depth, d_model = norm_mix_g.shape
    n_heads = sb_bias.shape[1]
    att = n_heads * HEAD_DIM
    conv = conv_w.shape[2]
    taps = conv_w.shape[1]
    n_pool, page = cache_k.shape[1], cache_k.shape[2]
    cache_kt = jnp.transpose(cache_k, (0, 1, 3, 4, 2)).reshape(depth, n_pool, att, page)
    cache_vt = jnp.transpose(cache_v, (0, 1, 3, 4, 2)).reshape(depth, n_pool, att, page)
    vec = lambda a: a.reshape(1, -1)
    final_g = vec(final_norm_g)
    bsz, seq, _ = x_prompt.shape

    def mix_in(i, x, a, u3, ctx):
        b, s, _ = u3.shape
        c3 = _conv_module(ctx, u3, conv_w[i], vec(conv_b[i]), vec(conv_norm_g[i]), vec(conv_norm_b[i]))
        wo = w_out[i].astype(BF16)
        mixed = (a, c3.reshape(b * s, conv), wo[:att], wo[att:])
        routed = None
        if i % 2:
            x, routed = _moe_dispatch(x, *mixed, vec(norm_ffn_g[i]), moe_router[i // 2])
        return x, mixed, routed, jnp.concatenate([ctx, u3], axis=1)[:, -(taps - 1):]

    def mix_out(i, x, mixed, routed, p3, after):
        j = i // 2
        if i % 2 == 0:
            x = _ffn(x, *mixed, vec(norm_ffn_g[i]), ffn_w_gate[j].astype(BF16), ffn_w_up[j].astype(BF16),
                     ffn_w_down[j].astype(BF16))
            adds = []
        else:
            adds = _moe_experts(*routed, moe_w_gate[j], moe_w_up[j], moe_w_down[j], after)
        return _ple(x, adds, p3.reshape(x.shape[0], -1), vec(norm_ple_g[i]), w_ple_gate[i].astype(BF16),
                    w_ple_proj[i].astype(BF16), final_g if i == depth - 1 else None)

    xp = x_prompt.reshape(bsz * seq, d_model)
    db, n_q, _ = x_sample.shape
    xs = x_sample.reshape(db * n_q, d_model)
    conv_zero = jnp.zeros((bsz, taps - 1, conv), F32)
    kv_t = None
    cp_l, ks_l, vs_l, cs_l = [], [], [], []
    for i in range(depth):
        xs_in = xs
        g = vec(norm_mix_g[i])
        w = w_in[i].astype(BF16)
        wkv_t = w_in[i][:, att:3 * att].T.astype(BF16)
        q, k_t, v_t, u = _inproj_kv_t(xp, g, w, wkv_t, att, conv, bsz, kv_t)
        kv_t = (k_t, v_t)
        a = _sb_prompt(q.reshape(bsz, seq, att), k_t, v_t, sb_bias[i]).reshape(bsz * seq, att)
        xp, mixed_p, routed_p, cp = mix_in(i, xp, a, u.reshape(bsz, seq, conv), conv_zero)

        q, k, v, u = _inproj(xs, g, w, att, conv)
        q3, k3, v3 = (t.reshape(db, n_q, att) for t in (q, k, v))
        a = _sb_paged(q3, k3, v3, cache_kt, cache_vt, i, page_table, sb_bias[i]).reshape(db * n_q, att)
        xs, mixed_s, routed_s, cs = mix_in(i, xs, a, u.reshape(db, n_q, conv), state_conv[i])

        xp = mix_out(i, xp, mixed_p, routed_p, p_prompt[i], xs_in)
        xs = mix_out(i, xs, mixed_s, routed_s, p_sample[i], xs_in)
        cp_l.append(cp)
        ks_l.append(k3.reshape(db, n_q, n_heads, HEAD_DIM))
        vs_l.append(v3.reshape(db, n_q, n_heads, HEAD_DIM))
        cs_l.append(cs)

    heads_last = lambda t: jnp.transpose(t.reshape(depth, bsz, n_heads, HEAD_DIM, seq), (0, 1, 4, 2, 3))
    return (xp.reshape(bsz, seq, d_model), xs.reshape(db, n_q, d_model), heads_last(kv_t[0]), heads_last(kv_t[1]),
            jnp.stack(cp_l), jnp.stack(ks_l), jnp.stack(vs_l), jnp.stack(cs_l))
```

```python
import functools
import math

import jax
import jax.numpy as jnp
from jax import lax
from jax.experimental import pallas as pl
from jax.experimental.pallas import tpu as pltpu

F32 = jnp.float32
BF16 = jnp.bfloat16
MX = jnp.bfloat16
EPS = 1e-6
HEAD_DIM = 64
LANES = 128
TOP_K = 2
VMEM_LIMIT = 56 * 1024 * 1024
PAGES_PER_STEP = 16
MOE_ROWS = 256
FF_CHUNK = 256


def _params(*sem):
    return pltpu.CompilerParams(dimension_semantics=sem, vmem_limit_bytes=VMEM_LIMIT)


def _tile(n, pref):
    if n <= pref:
        return n
    t = pref - pref % 8
    while n % t:
        t -= 8
    return t


def _pallas(body, *, out_shape, flops, transcendentals=0, bytes_accessed=None, **kw):
    def run(*args):
        outs = out_shape if isinstance(out_shape, (list, tuple)) else [out_shape]
        nbytes = bytes_accessed
        if nbytes is None:
            nbytes = sum(math.prod(a.shape) * jnp.dtype(a.dtype).itemsize for a in (*args, *outs))
        cost = pl.CostEstimate(flops=int(flops), transcendentals=int(transcendentals), bytes_accessed=int(nbytes))
        return pl.pallas_call(body, out_shape=out_shape, cost_estimate=cost, **kw)(*args)
    return run


def _dot(a, b):
    return jnp.dot(a, b, preferred_element_type=F32)


def _dot_nt(a, b):
    return lax.dot_general(a, b, (((1,), (1,)), ((), ())), preferred_element_type=F32)


def _rms(xf, g):
    return xf * lax.rsqrt(jnp.mean(xf * xf, axis=-1, keepdims=True) + EPS) * g


def _sigmoid(x):
    return 1.0 / (1.0 + jnp.exp(-x))


def _split_bf16(x):
    hi = x.astype(BF16)
    lo = (x - hi.astype(F32)).astype(BF16)
    return hi, lo


def _inproj_kernel(*refs, att, conv, kv_t, n_prev=0):
    x_ref, g_ref, w_ref = refs[:3]
    q_ref, k_ref, v_ref, u_ref = refs[-4:]
    hn = _rms(x_ref[...], g_ref[...]).astype(BF16)
    col = lambda c0, n: _dot(hn, w_ref[:, c0:c0 + n])
    q_ref[...] = col(0, att)
    if kv_t:
        kv = _dot_nt(refs[3][...], hn)
        k_ref[n_prev] = kv[:att]
        v_ref[n_prev] = kv[att:]
        if n_prev:
            k_ref[0:n_prev] = refs[4][...]
            v_ref[0:n_prev] = refs[5][...]
    else:
        k_ref[...] = col(att, att)
        v_ref[...] = col(2 * att, att)
    a = col(3 * att, conv)
    g = col(3 * att + conv, conv)
    u_ref[...] = a * _sigmoid(g)


def _inproj(x, g, w, att, conv):
    t, d = x.shape
    tm = _tile(t, 512)
    row = lambda n: pl.BlockSpec((tm, n), lambda i: (i, 0))
    full = lambda a: pl.BlockSpec(a.shape, lambda i: (0, 0))
    return _pallas(
        functools.partial(_inproj_kernel, att=att, conv=conv, kv_t=False),
        flops=2 * t * d * w.shape[1], transcendentals=t * conv,
        grid=(t // tm,),
        in_specs=[row(d), full(g), full(w)],
        out_specs=[row(att), row(att), row(att), row(conv)],
        out_shape=[jax.ShapeDtypeStruct((t, att), F32)] * 3 + [jax.ShapeDtypeStruct((t, conv), F32)],
        compiler_params=_params("parallel"),
        name="inproj",
    )(x, g, w)


def _inproj_kv_t(x, g, w, wkv_t, att, conv, batch, kv_prev):
    t, d = x.shape
    seq = t // batch
    tm = _tile(seq, 512)
    per_seq = seq // tm
    prev = list(kv_prev) if kv_prev is not None else []
    n_prev = prev[0].shape[0] if prev else 0
    row = lambda n: pl.BlockSpec((tm, n), lambda i: (i, 0))
    full = lambda a: pl.BlockSpec(a.shape, lambda i: (0, 0))
    slabs = lambda n: pl.BlockSpec((n, None, att, tm), lambda i: (0, i // per_seq, 0, i % per_seq))
    kv_shape = jax.ShapeDtypeStruct((n_prev + 1, batch, att, seq), F32)
    return _pallas(
        functools.partial(_inproj_kernel, att=att, conv=conv, kv_t=True, n_prev=n_prev),
        flops=2 * t * d * (w.shape[1] - 2 * att + wkv_t.shape[0]), transcendentals=t * conv,
        grid=(t // tm,),
        in_specs=[row(d), full(g), full(w), full(wkv_t)] + [slabs(n_prev)] * len(prev),
        out_specs=[row(att), slabs(n_prev + 1), slabs(n_prev + 1), row(conv)],
        out_shape=[jax.ShapeDtypeStruct((t, att), F32), kv_shape, kv_shape, jax.ShapeDtypeStruct((t, conv), F32)],
        compiler_params=_params("parallel"),
        name="inproj_kv_t",
    )(x, g, w, wkv_t, *prev)


def _log_sigmoids(z):
    ls = jnp.minimum(z, 0.0) - jnp.log(1.0 + jnp.exp(-jnp.abs(z)))
    return ls, ls - z


def _later_than(n):
    j = lax.broadcasted_iota(jnp.int32, (n, n), 0)
    s = lax.broadcasted_iota(jnp.int32, (n, n), 1)
    return jnp.where(j > s, 1.0, 0.0).astype(MX)


def _sb_weights(z, tri, carry, mask):
    ls, lk = _log_sigmoids(z)
    if mask is not None:
        lk = jnp.where(mask, lk, 0.0)
    later = _dot(lk.astype(MX), tri)
    a = jnp.exp(ls + later + carry)
    if mask is not None:
        a = jnp.where(mask, a, 0.0)
    return a, carry + later[:, 0:1] + lk[:, 0:1]


def _sb_prompt_kernel(bias_ref, q_ref, k_ref, v_ref, o_ref, z_ref, a_ref, carry_ref, *, tq):
    hp = pl.program_id(1)
    qi = pl.program_id(2)
    first = lax.broadcasted_iota(jnp.int32, (tq, LANES), 1) < HEAD_DIM
    q = q_ref[...] * (HEAD_DIM ** -0.5)
    q2 = jnp.concatenate([jnp.where(first, q, 0.0), jnp.where(first, 0.0, q)], axis=0).astype(MX)
    bias_a = bias_ref[2 * hp]
    bias_b = bias_ref[2 * hp + 1]
    tri = _later_than(tq)
    r = lax.broadcasted_iota(jnp.int32, (2 * tq, tq), 0)
    c = lax.broadcasted_iota(jnp.int32, (2 * tq, tq), 1)
    causal = c < jnp.where(r < tq, r, r - tq)
    v_first = lax.broadcasted_iota(jnp.int32, (LANES, tq), 0) < HEAD_DIM

    def cols(j):
        return pl.ds(pl.multiple_of(j * tq, tq), tq)

    def logits(j):
        z = _dot(q2, k_ref[:, cols(j)].astype(MX))
        return jnp.concatenate([z[:tq] + bias_a, z[tq:] + bias_b], axis=0)

    def weighted_values(a2, j):
        vb = v_ref[:, cols(j)]
        v2 = jnp.concatenate([jnp.where(v_first, vb, 0.0), jnp.where(v_first, 0.0, vb)], axis=1).astype(MX)
        return _dot_nt(a2, v2)

    def keep(a, carry):
        a_ref[...] = jnp.concatenate([a[:tq], a[tq:]], axis=1).astype(MX)
        carry_ref[...] = carry

    keep(*_sb_weights(logits(qi), tri, jnp.zeros((2 * tq, 1), F32), causal))
    z_ref[...] = logits(jnp.maximum(qi - 1, 0))
    o_ref[...] = jnp.zeros((tq, LANES), F32)

    def body(n, c):
        j = qi - 1 - n
        o_ref[...] += weighted_values(a_ref[...], j + 1)
        z = z_ref[...]
        z_ref[...] = logits(jnp.maximum(j - 1, 0))
        keep(*_sb_weights(z, tri, carry_ref[...], None))
        return c

    lax.fori_loop(0, qi, body, 0)
    o_ref[...] += weighted_values(a_ref[...], 0)


def _sb_prompt(q, k_t, v_t, bias):
    b, s, w = q.shape
    tq = _tile(s, 256)
    layer = k_t.shape[0] - 1
    qspec = pl.BlockSpec((None, tq, LANES), lambda bi, hp, qi: (bi, qi, hp))
    kvspec = pl.BlockSpec((None, None, LANES, s), lambda bi, hp, qi: (layer, bi, hp, 0))
    tiles = b * (w // LANES) * (s // tq) * (s // tq + 1) // 2
    return _pallas(
        functools.partial(_sb_prompt_kernel, tq=tq),
        flops=tiles * (8 * tq * tq * LANES + 4 * tq * tq * tq), transcendentals=tiles * 6 * tq * tq,
        bytes_accessed=4 * b * s * w * 4,
        grid=(b, w // LANES, s // tq),
        in_specs=[pl.BlockSpec(memory_space=pltpu.SMEM), qspec, kvspec, kvspec],
        out_specs=qspec,
        out_shape=jax.ShapeDtypeStruct((b, s, w), F32),
        scratch_shapes=[pltpu.VMEM((2 * tq, tq), F32), pltpu.VMEM((tq, 2 * tq), MX),
                        pltpu.VMEM((2 * tq, 1), F32)],
        compiler_params=_params("parallel", "parallel", "arbitrary"),
        name="sb_prompt",
    )(bias, q, k_t, v_t)


def _sb_paged_kernel(pt_ref, bias_ref, q_ref, kn_ref, vn_ref, *rest, n_heads, n_q, page):
    kpages = rest[:PAGES_PER_STEP]
    vpages = rest[PAGES_PER_STEP:2 * PAGES_PER_STEP]
    o_ref, qbd_ref, acc_ref, new_ref, carry_ref, w_ref = rest[2 * PAGES_PER_STEP:]
    step = pl.program_id(1)
    rows = n_q * n_heads
    width = n_heads * HEAD_DIM
    pairs = PAGES_PER_STEP // 2
    bias = bias_ref[...]
    tri = _later_than(2 * page)

    @pl.when(step == 0)
    def _init():
        q = q_ref[...] * (HEAD_DIM ** -0.5)
        rep = jnp.concatenate([jnp.broadcast_to(q[t:t + 1, :], (n_heads, width)) for t in range(n_q)], axis=0)
        row = lax.broadcasted_iota(jnp.int32, (rows, width), 0)
        lane = lax.broadcasted_iota(jnp.int32, (rows, width), 1)
        qbd = jnp.where(lane // HEAD_DIM == row % n_heads, rep, 0.0)
        qbd_ref[...] = qbd
        pad = jnp.zeros((page - n_q, width), F32)
        kn = jnp.concatenate([kn_ref[...], pad], axis=0)
        vn = jnp.concatenate([vn_ref[...], pad], axis=0)
        key = lax.broadcasted_iota(jnp.int32, (rows, page), 1)
        t = lax.broadcasted_iota(jnp.int32, (rows, page), 0) // n_heads
        z = _dot_nt(qbd.astype(MX), kn.astype(MX)) + bias
        a, carry = _sb_weights(z, tri[:page, :page], jnp.zeros((rows, 1), F32), key < t)
        new_ref[...] = _dot(a.astype(MX), vn.astype(MX))
        carry_ref[...] = carry
        acc_ref[...] = jnp.zeros(acc_ref.shape, F32)
        w_ref[...] = jnp.zeros(w_ref.shape, MX)

    w_prev = w_ref[...]
    qbd = qbd_ref[...].astype(MX)
    scores = []
    for i in range(pairs):
        keys = jnp.concatenate([kpages[2 * i][...], kpages[2 * i + 1][...]], axis=1)
        scores.append(_log_sigmoids(_dot(qbd, keys.astype(MX)) + bias))
    later = _dot(jnp.concatenate([lk for _, lk in scores], axis=0).astype(MX), tri)
    carry = carry_ref[...]
    weights = [None] * pairs
    for i in reversed(range(pairs)):
        ls, lk = scores[i]
        lt = later[i * rows:(i + 1) * rows]
        weights[i] = jnp.exp(ls + lt + carry)
        carry = carry + lt[:, 0:1] + lk[:, 0:1]
    carry_ref[...] = carry
    w_ref[...] = jnp.concatenate(weights, axis=1).astype(MX)
    values = jnp.concatenate([vpages[p][...] for p in range(PAGES_PER_STEP)], axis=1)
    acc_ref[...] += _dot_nt(values.astype(MX), w_prev)

    @pl.when(step == pl.num_programs(1) - 1)
    def _finish():
        r = lax.broadcasted_iota(jnp.int32, (width, rows), 0)
        c = lax.broadcasted_iota(jnp.int32, (width, rows), 1)
        hi, lo = _split_bf16(jnp.where(c % n_heads == r // HEAD_DIM, acc_ref[...], 0.0))
        t = lax.broadcasted_iota(jnp.int32, (8, rows), 0)
        cc = lax.broadcasted_iota(jnp.int32, (8, rows), 1)
        pick = jnp.where(cc // n_heads == t, 1.0, 0.0).astype(BF16)
        past = _dot_nt(pick, hi) + _dot_nt(pick, lo)
        row = lax.broadcasted_iota(jnp.int32, (rows, width), 0)
        lane = lax.broadcasted_iota(jnp.int32, (rows, width), 1)
        own = jnp.where(lane // HEAD_DIM == row % n_heads, new_ref[...], 0.0)
        o_ref[...] = past[:n_q] + jnp.sum(own.reshape(n_q, n_heads, width), axis=1)


def _sb_paged(q, k_new, v_new, cache_kt, cache_vt, layer, page_table, bias):
    db, n_q, width = q.shape
    n_heads = width // HEAD_DIM
    page = cache_kt.shape[3]
    n_pages = page_table.shape[1]
    steps = n_pages // PAGES_PER_STEP
    rows = n_q * n_heads
    bias_col = jnp.tile(bias, n_q).reshape(rows, 1)

    def page_spec(p, delay):
        def index(bi, si, pt):
            group = steps - 1 - jnp.clip(si - delay, 0, steps - 1)
            return (layer, pt[bi, group * PAGES_PER_STEP + p], 0, 0)
        return pl.BlockSpec((None, None, width, page), index)

    small = pl.BlockSpec((None, n_q, width), lambda bi, si, pt: (bi, 0, 0))
    kpages = [page_spec(p, 0) for p in range(PAGES_PER_STEP)]
    vpages = [page_spec(p, 1) for p in range(PAGES_PER_STEP)]
    grid_spec = pltpu.PrefetchScalarGridSpec(
        num_scalar_prefetch=1,
        grid=(db, steps + 1),
        in_specs=[pl.BlockSpec((rows, 1), lambda bi, si, pt: (0, 0)), small, small, small] + kpages + vpages,
        out_specs=small,
        scratch_shapes=[pltpu.VMEM((rows, width), F32), pltpu.VMEM((width, rows), F32),
                        pltpu.VMEM((rows, width), F32), pltpu.VMEM((rows, 1), F32),
                        pltpu.VMEM((rows, PAGES_PER_STEP * page), MX)],
    )
    keys = n_pages * page
    return _pallas(
        functools.partial(_sb_paged_kernel, n_heads=n_heads, n_q=n_q, page=page),
        flops=db * keys * rows * (4 * width + 4 * page), transcendentals=db * keys * rows * 3,
        bytes_accessed=2 * db * keys * width * 4,
        grid_spec=grid_spec,
        out_shape=jax.ShapeDtypeStruct((db, n_q, width), F32),
        compiler_params=_params("parallel", "arbitrary"),
        name="sb_paged",
    )(page_table, bias_col, q, k_new, v_new, *([cache_kt] * PAGES_PER_STEP), *([cache_vt] * PAGES_PER_STEP))


CONV_PAD = 32


def _conv_kernel(ctx_ref, u_ref, w_ref, b_ref, g_ref, nb_ref, o_ref, ext_ref, *, taps, seq, rows):
    n_ctx = taps - 1
    off = CONV_PAD - n_ctx
    ext_ref[0:CONV_PAD, :] = jnp.zeros((CONV_PAD, ext_ref.shape[1]), F32)
    ext_ref[off:CONV_PAD, :] = ctx_ref[...]
    if seq % 8:
        ext_ref[CONV_PAD:, :] = jnp.zeros((ext_ref.shape[0] - CONV_PAD, ext_ref.shape[1]), F32)
    ext_ref[CONV_PAD:CONV_PAD + seq, :] = u_ref[...]
    w = w_ref[...]
    out_rows = min(rows, seq)

    def chunk(i, _):
        base = pl.multiple_of(i * rows, rows)
        span = rows + CONV_PAD
        parts = []
        for c0 in range(0, ext_ref.shape[1], LANES):
            win = ext_ref[pl.ds(base, span), c0:c0 + LANES]
            part = jnp.zeros((rows, LANES), F32) + b_ref[:, c0:c0 + LANES]
            for s in range(8):
                shifted = win if s == 0 else pltpu.roll(win, span - s, axis=0)
                for k in range(taps):
                    o = off + k - s
                    if o % 8 == 0:
                        part = part + w[k:k + 1, c0:c0 + LANES] * shifted[o:o + rows]
            parts.append(part)
        acc = jnp.concatenate(parts, axis=1)
        mu = jnp.mean(acc, axis=-1, keepdims=True)
        xc = acc - mu
        y = xc * lax.rsqrt(jnp.mean(xc * xc, axis=-1, keepdims=True) + EPS) * g_ref[...] + nb_ref[...]
        y = y * _sigmoid(y)
        o_ref[pl.ds(base, out_rows), :] = y[:out_rows]
        return 0

    lax.fori_loop(0, max(seq // rows, 1), chunk, 0)


def _conv_module(ctx, u, w, b, g, nb):
    bsz, seq, ch = u.shape
    taps = w.shape[0]
    rows = 128 if seq >= 128 else 8
    ext_rows = CONV_PAD + max(seq, rows)
    vec = lambda a: pl.BlockSpec(a.shape, lambda i: (0, 0))
    return _pallas(
        functools.partial(_conv_kernel, taps=taps, seq=seq, rows=rows),
        flops=2 * taps * bsz * seq * ch, transcendentals=bsz * seq * ch,
        grid=(bsz,),
        in_specs=[pl.BlockSpec((None, taps - 1, ch), lambda i: (i, 0, 0)),
                  pl.BlockSpec((None, seq, ch), lambda i: (i, 0, 0)),
                  vec(w), vec(b), vec(g), vec(nb)],
        out_specs=pl.BlockSpec((None, seq, ch), lambda i: (i, 0, 0)),
        out_shape=jax.ShapeDtypeStruct((bsz, seq, ch), F32),
        scratch_shapes=[pltpu.VMEM((ext_rows, ch), F32)],
        compiler_params=_params("parallel"),
        name="conv_module",
    )(ctx, u, w, b, g, nb)


def _mixed_residual(x_ref, att_ref, conv_ref, wa_ref, wc_ref):
    return (x_ref[...] + _dot(att_ref[...].astype(BF16), wa_ref[...])
            + _dot(conv_ref[...].astype(BF16), wc_ref[...]))


def _mixed_specs(tm, x, att, conv, wa, wc):
    row = lambda a: pl.BlockSpec((tm, a.shape[1]), lambda i: (i, 0))
    full = lambda a: pl.BlockSpec(a.shape, lambda i: (0, 0))
    return [row(x), row(att), row(conv), full(wa), full(wc)]


def _swiglu_hidden(hn, wg_ref, wu_ref, h_ref):
    d_ff = h_ref.shape[1]
    for c0 in range(0, d_ff, FF_CHUNK):
        g = _dot(hn, wg_ref[:, c0:c0 + FF_CHUNK].astype(BF16))
        u = _dot(hn, wu_ref[:, c0:c0 + FF_CHUNK].astype(BF16))
        h_ref[:, c0:c0 + FF_CHUNK] = (g * _sigmoid(g) * u).astype(BF16)


def _ffn_kernel(x_ref, att_ref, conv_ref, wa_ref, wc_ref, g_ref, wg_ref, wu_ref, wd_ref, o_ref, h_ref):
    x = _mixed_residual(x_ref, att_ref, conv_ref, wa_ref, wc_ref)
    _swiglu_hidden(_rms(x, g_ref[...]).astype(BF16), wg_ref, wu_ref, h_ref)
    o_ref[...] = x + _dot(h_ref[...], wd_ref[...])


def _resident(shape, index):
    return pl.BlockSpec(shape, index, pipeline_mode=pl.Buffered(1))


def _ffn(x, att, conv, wa, wc, g, wg, wu, wd):
    t, d = x.shape
    d_ff = wg.shape[1]
    tm = _tile(t, 512)
    return _pallas(
        _ffn_kernel,
        flops=2 * t * d * d + 6 * t * d * d_ff, transcendentals=t * d_ff,
        grid=(t // tm,),
        in_specs=_mixed_specs(tm, x, att, conv, wa, wc) + [
            pl.BlockSpec(g.shape, lambda i: (0, 0)),
            _resident(wg.shape, lambda i: (0, 0)), _resident(wu.shape, lambda i: (0, 0)),
            _resident(wd.shape, lambda i: (0, 0))],
        out_specs=pl.BlockSpec((tm, d), lambda i: (i, 0)),
        out_shape=jax.ShapeDtypeStruct((t, d), F32),
        scratch_shapes=[pltpu.VMEM((tm, d_ff), BF16)],
        compiler_params=_params("parallel"),
        name="ffn_dense",
    )(x, att, conv, wa, wc, g, wg, wu, wd)


def _expert_kernel(be_ref, x_ref, wg_ref, wu_ref, wd_ref, after_ref, o_ref, h_ref):
    _swiglu_hidden(x_ref[...], wg_ref, wu_ref, h_ref)
    o_ref[...] = _dot(h_ref[...], wd_ref[...].astype(BF16))


def _expert_blocks(xg, block_e, wg, wu, wd, after):
    rows, d = xg.shape
    d_ff = wg.shape[2]
    grid_spec = pltpu.PrefetchScalarGridSpec(
        num_scalar_prefetch=1,
        grid=(rows // MOE_ROWS,),
        in_specs=[pl.BlockSpec((MOE_ROWS, d), lambda i, be: (i, 0)),
                  _resident((None, d, d_ff), lambda i, be: (be[i], 0, 0)),
                  _resident((None, d, d_ff), lambda i, be: (be[i], 0, 0)),
                  _resident((None, d_ff, d), lambda i, be: (be[i], 0, 0)),
                  pl.BlockSpec((8, LANES), lambda i, be: (0, 0))],
        out_specs=pl.BlockSpec((MOE_ROWS, d), lambda i, be: (i, 0)),
        scratch_shapes=[pltpu.VMEM((MOE_ROWS, d_ff), BF16)],
    )
    return _pallas(
        _expert_kernel,
        flops=6 * rows * d * d_ff, transcendentals=rows * d_ff,
        grid_spec=grid_spec,
        out_shape=jax.ShapeDtypeStruct((rows, d), F32),
        compiler_params=_params("arbitrary"),
        name="moe_experts",
    )(block_e, xg, wg, wu, wd, after)


def _router_kernel(x_ref, att_ref, conv_ref, wa_ref, wc_ref, g_ref, r_ref, xo_ref, hn_ref, route_ref, *, n_experts):
    x = _mixed_residual(x_ref, att_ref, conv_ref, wa_ref, wc_ref)
    xo_ref[...] = x
    hn = _rms(x, g_ref[...])
    hn_ref[...] = hn.astype(BF16)
    h_hi, h_lo = _split_bf16(hn)
    r_hi, r_lo = _split_bf16(r_ref[...])
    logits = _dot(h_hi, r_hi) + (_dot(h_hi, r_lo) + _dot(h_lo, r_hi))
    lane = lax.broadcasted_iota(jnp.int32, logits.shape, 1).astype(F32)
    neg = jnp.float32(-jnp.inf)
    logits = jnp.where(lane < n_experts, logits, neg)
    v1 = jnp.max(logits, axis=-1, keepdims=True)
    i1 = jnp.min(jnp.where(logits == v1, lane, float(LANES)), axis=-1, keepdims=True)
    rest = jnp.where(lane == i1, neg, logits)
    v2 = jnp.max(rest, axis=-1, keepdims=True)
    i2 = jnp.min(jnp.where(rest == v2, lane, float(LANES)), axis=-1, keepdims=True)
    e = jnp.exp(v2 - v1)
    g1 = 1.0 / (1.0 + e)
    g2 = e / (1.0 + e)
    route_ref[...] = jnp.where(lane == 0, i1, jnp.where(lane == 1, i2, jnp.where(lane == 2, g1,
                               jnp.where(lane == 3, g2, 0.0))))


def _router(x, att, conv, wa, wc, g, router_pad, n_experts):
    t, d = x.shape
    tm = _tile(t, 512)
    row = lambda n: pl.BlockSpec((tm, n), lambda i: (i, 0))
    return _pallas(
        functools.partial(_router_kernel, n_experts=n_experts),
        flops=2 * t * d * d + 6 * t * d * LANES, transcendentals=t,
        grid=(t // tm,),
        in_specs=_mixed_specs(tm, x, att, conv, wa, wc) + [
            pl.BlockSpec(g.shape, lambda i: (0, 0)), pl.BlockSpec(router_pad.shape, lambda i: (0, 0))],
        out_specs=[row(d), row(d), row(LANES)],
        out_shape=[jax.ShapeDtypeStruct((t, d), F32), jax.ShapeDtypeStruct((t, d), BF16),
                   jax.ShapeDtypeStruct((t, LANES), F32)],
        compiler_params=_params("parallel"),
        name="router",
    )(x, att, conv, wa, wc, g, router_pad)


def _moe_route(x, att, conv, wa, wc, g, router):
    n_experts = router.shape[1]
    router_pad = jnp.pad(router, ((0, 0), (0, LANES - n_experts)))
    return _router(x, att, conv, wa, wc, g, router_pad, n_experts)


def _moe_dispatch(hn, route, n_experts):
    t, d = hn.shape
    flat_e = route[:, :TOP_K].astype(jnp.int32).reshape(-1)
    n_assign = t * TOP_K
    onehot = (flat_e[:, None] == jnp.arange(n_experts, dtype=jnp.int32)[None, :]).astype(jnp.int32)
    before = jnp.cumsum(onehot, axis=0) - onehot
    counts = jnp.sum(onehot, axis=0)
    padded = (counts + MOE_ROWS - 1) // MOE_ROWS * MOE_ROWS
    pad_end = jnp.cumsum(padded)
    pad_start = pad_end - padded
    dest = jnp.sum(onehot * (before + pad_start[None, :]), axis=1)
    n_blocks = -(-n_assign // MOE_ROWS) + n_experts
    rows = n_blocks * MOE_ROWS
    row_tok = jnp.zeros((rows,), jnp.int32).at[dest].set(
        jnp.arange(n_assign, dtype=jnp.int32) // TOP_K, unique_indices=True, mode='promise_in_bounds')
    block_start = jnp.arange(n_blocks, dtype=jnp.int32) * MOE_ROWS
    block_e = jnp.minimum(jnp.sum((pad_end[None, :] <= block_start[:, None]).astype(jnp.int32), axis=1),
                          n_experts - 1)
    return hn[row_tok], block_e, dest.reshape(t, TOP_K)


def _moe_experts(xg, block_e, pos, routes, wg, wu, wd, after):
    y = _expert_blocks(xg, block_e, wg, wu, wd, after)
    out, t0 = [], 0
    for route in routes:
        p = pos[t0:t0 + route.shape[0]]
        out.append([y[p[:, 0]], y[p[:, 1]], route])
        t0 += route.shape[0]
    return out


def _ple_kernel(*refs, n_add, final):
    x_ref = refs[0]
    p_ref, g_ref, wgate_ref, wproj_ref = refs[1 + n_add:5 + n_add]
    fg_ref = refs[5 + n_add] if final else None
    o_ref = refs[-1]
    x = x_ref[...]
    if n_add:
        ya_ref, yb_ref, route_ref = refs[1:4]
        route = route_ref[...]
        x = x + (ya_ref[...] * route[:, TOP_K:TOP_K + 1] + yb_ref[...] * route[:, TOP_K + 1:TOP_K + 2])
    gate = _sigmoid(_dot(_rms(x, g_ref[...]).astype(BF16), wgate_ref[...]))
    x = x + gate * _dot(p_ref[...].astype(BF16), wproj_ref[...])
    o_ref[...] = _rms(x, fg_ref[...]) if final else x


def _ple(x, adds, p, g, wgate, wproj, final_g):
    t, d = x.shape
    tm = _tile(t, 512)
    row = lambda a: pl.BlockSpec((tm, a.shape[1]), lambda i: (i, 0))
    full = lambda a: pl.BlockSpec(a.shape, lambda i: (0, 0))
    final = final_g is not None
    ops = [x, *adds, p, g, wgate, wproj] + ([final_g] if final else [])
    specs = [row(x)] + [row(a) for a in adds] + [row(p), full(g), full(wgate), full(wproj)]
    specs += [full(final_g)] if final else []
    return _pallas(
        functools.partial(_ple_kernel, n_add=len(adds), final=final),
        flops=2 * t * d * d + 2 * t * p.shape[1] * d, transcendentals=t * d,
        grid=(t // tm,),
        in_specs=specs,
        out_specs=row(x),
        out_shape=jax.ShapeDtypeStruct((t, d), F32),
        compiler_params=_params("parallel"),
        name="ple",
    )(*ops)


def kernel(x_prompt, x_sample, cache_k, cache_v, state_conv, page_table, p_prompt, p_sample,
           w_in, sb_bias, w_out, conv_w, conv_b, conv_norm_g, conv_norm_b, norm_mix_g, norm_ffn_g,
           norm_ple_g, w_ple_gate, w_ple_proj, ffn_w_gate, ffn_w_up, ffn_w_down,
           moe_router, moe_w_gate, moe_w_up, moe_w_down, final_norm_g):
    depth, d_model = norm_mix_g.shape
    n_heads = sb_bias.shape[1]
    att = n_heads * HEAD_DIM
    conv = conv_w.shape[2]
    taps = conv_w.shape[1]
    n_pool, page = cache_k.shape[1], cache_k.shape[2]
    cache_kt = jnp.transpose(cache_k, (0, 1, 3, 4, 2)).reshape(depth, n_pool, att, page)
    cache_vt = jnp.transpose(cache_v, (0, 1, 3, 4, 2)).reshape(depth, n_pool, att, page)
    vec = lambda a: a.reshape(1, -1)
    final_g = vec(final_norm_g)
    bsz, seq, _ = x_prompt.shape

    def mix_in(i, x, a, u3, ctx):
        b, s, _ = u3.shape
        c3 = _conv_module(ctx, u3, conv_w[i], vec(conv_b[i]), vec(conv_norm_g[i]), vec(conv_norm_b[i]))
        wo = w_out[i].astype(BF16)
        mixed = (a, c3.reshape(b * s, conv), wo[:att], wo[att:])
        routed = None
        if i % 2:
            x, hn, route = _moe_route(x, *mixed, vec(norm_ffn_g[i]), moe_router[i // 2])
            routed = (hn, route)
        return x, mixed, routed, jnp.concatenate([ctx, u3], axis=1)[:, -(taps - 1):]

    def mix_out(i, x, mixed, adds, p3):
        j = i // 2
        if i % 2 == 0:
            x = _ffn(x, *mixed, vec(norm_ffn_g[i]), ffn_w_gate[j].astype(BF16), ffn_w_up[j].astype(BF16),
                     ffn_w_down[j].astype(BF16))
        return _ple(x, adds, p3.reshape(x.shape[0], -1), vec(norm_ple_g[i]), w_ple_gate[i].astype(BF16),
                    w_ple_proj[i].astype(BF16), final_g if i == depth - 1 else None)

    xp = x_prompt.reshape(bsz * seq, d_model)
    db, n_q, _ = x_sample.shape
    xs = x_sample.reshape(db * n_q, d_model)
    conv_zero = jnp.zeros((bsz, taps - 1, conv), F32)
    kv_t = None
    cp_l, ks_l, vs_l, cs_l = [], [], [], []
    for i in range(depth):
        xs_in = xs
        g = vec(norm_mix_g[i])
        w = w_in[i].astype(BF16)
        wkv_t = w_in[i][:, att:3 * att].T.astype(BF16)
        q, k_t, v_t, u = _inproj_kv_t(xp, g, w, wkv_t, att, conv, bsz, kv_t)
        kv_t = (k_t, v_t)
        a = _sb_prompt(q.reshape(bsz, seq, att), k_t, v_t, sb_bias[i]).reshape(bsz * seq, att)
        xp, mixed_p, routed_p, cp = mix_in(i, xp, a, u.reshape(bsz, seq, conv), conv_zero)

        q, k, v, u = _inproj(xs, g, w, att, conv)
        q3, k3, v3 = (t.reshape(db, n_q, att) for t in (q, k, v))
        a = _sb_paged(q3, k3, v3, cache_kt, cache_vt, i, page_table, sb_bias[i]).reshape(db * n_q, att)
        xs, mixed_s, routed_s, cs = mix_in(i, xs, a, u.reshape(db, n_q, conv), state_conv[i])

        adds_p, adds_s = [], []
        if i % 2:
            j = i // 2
            hn_all = jnp.concatenate([routed_p[0], routed_s[0]], axis=0)
            route_all = jnp.concatenate([routed_p[1], routed_s[1]], axis=0)
            grouped = _moe_dispatch(hn_all, route_all, moe_router.shape[2])
            adds_p, adds_s = _moe_experts(*grouped, (routed_p[1], routed_s[1]), moe_w_gate[j], moe_w_up[j],
                                          moe_w_down[j], xs_in)
        xp = mix_out(i, xp, mixed_p, adds_p, p_prompt[i])
        xs = mix_out(i, xs, mixed_s, adds_s, p_sample[i])
        cp_l.append(cp)
        ks_l.append(k3.reshape(db, n_q, n_heads, HEAD_DIM))
        vs_l.append(v3.reshape(db, n_q, n_heads, HEAD_DIM))
        cs_l.append(cs)

    heads_last = lambda t: jnp.transpose(t.reshape(depth, bsz, n_heads, HEAD_DIM, seq), (0, 1, 4, 2, 3))
    return (xp.reshape(bsz, seq, d_model), xs.reshape(db, n_q, d_model), heads_last(kv_t[0]), heads_last(kv_t[1]),
            jnp.stack(cp_l), jnp.stack(ks_l), jnp.stack(vs_l), jnp.stack(cs_l))
```
